```python
import math
import jax, jax.numpy as jnp
from jax import lax
import numpy as np

D_MODEL = 2048
BATCH = 4
SEQ = 4096
DEPTH = 1

MEM_LEN = 256

POOL_WIDTH = D_MODEL // 2
POOL_WINDOWS = (2, 4, 8, 16)
POOL_GROUPS = len(POOL_WINDOWS)
POOL_GROUP_DIM = POOL_WIDTH // POOL_GROUPS

SGU_WIDTH = D_MODEL // 2
SGU_CHUNK = 128
SGU_HEADS = 8
SGU_HEAD_DIM = SGU_WIDTH // SGU_HEADS

XATTN_HEADS = 4
XATTN_HEAD_DIM = D_MODEL // 8
XATTN_WIDTH = XATTN_HEADS * XATTN_HEAD_DIM

BRANCH_WIDTHS = (POOL_WIDTH, SGU_WIDTH, XATTN_WIDTH)
MIX_WIDTH = POOL_WIDTH + SGU_WIDTH + XATTN_WIDTH
IN_SPLITS = (POOL_WIDTH, POOL_WIDTH, SGU_WIDTH, SGU_WIDTH, SGU_WIDTH, XATTN_WIDTH, XATTN_WIDTH)
IN_WIDTH = sum(IN_SPLITS)
EPS = 1e-6

kernel_name = "hybrid_pool_sgu_memxattn_layer"


def rmsnorm(x, g):
    xf = x.astype(jnp.float32)
    y = xf * lax.rsqrt(jnp.mean(xf * xf, axis=-1, keepdims=True) + EPS)
    return (y * g.astype(jnp.float32)).astype(x.dtype)


def layernorm(x, g, b):
    xf = x.astype(jnp.float32)
    mu = jnp.mean(xf, axis=-1, keepdims=True)
    xc = xf - mu
    y = xc * lax.rsqrt(jnp.mean(xc * xc, axis=-1, keepdims=True) + EPS)
    return (y * g.astype(jnp.float32) + b.astype(jnp.float32)).astype(x.dtype)


def split_cols(a, sizes):
    idx = list(np.cumsum(sizes)[:-1])
    return jnp.split(a, idx, axis=-1)


def pool_mixer(xa, w_pool, scale):
    B, S, _ = xa.shape
    xg = xa.reshape(B, S, POOL_GROUPS, POOL_GROUP_DIM).astype(jnp.float32)
    csum = jnp.cumsum(xg, axis=1)
    t = jnp.arange(1, S + 1, dtype=jnp.float32)
    outs = []
    for g, w in enumerate(POOL_WINDOWS):
        cg = csum[:, :, g]
        lower = jnp.pad(cg[:, :S - w], ((0, 0), (w, 0), (0, 0)))
        count = jnp.minimum(t, float(w))[None, :, None]
        outs.append((cg - lower) / count - xg[:, :, g])
    d = jnp.stack(outs, axis=2).astype(xa.dtype)
    y = jnp.einsum('bsgc,gcd->bsgd', d, w_pool)
    return y.reshape(B, S, POOL_WIDTH) * scale


def spatial_gating(u, v, ln_g, ln_b, w_s, b_s):
    B, S, _ = v.shape
    n_chunks = S // SGU_CHUNK
    vn = layernorm(v, ln_g, ln_b)
    vc = vn.reshape(B, n_chunks, SGU_CHUNK, SGU_HEADS, SGU_HEAD_DIM)
    causal = jnp.tril(jnp.ones((SGU_CHUNK, SGU_CHUNK), dtype=bool))
    w = jnp.where(causal[None], w_s, jnp.zeros_like(w_s))
    z = jnp.einsum('hts,bnshd->bnthd', w, vc) + jnp.transpose(b_s)[None, None, :, :, None]
    return u * z.reshape(B, S, SGU_WIDTH)


def memory_cross_attention(q, k, v):
    B, S, _ = q.shape
    M = k.shape[1]
    qh = q.reshape(B, S, XATTN_HEADS, XATTN_HEAD_DIM)
    kh = k.reshape(B, M, XATTN_HEADS, XATTN_HEAD_DIM)
    vh = v.reshape(B, M, XATTN_HEADS, XATTN_HEAD_DIM)
    s = jnp.einsum('bshd,bmhd->bhsm', qh, kh).astype(jnp.float32) * (1.0 / math.sqrt(XATTN_HEAD_DIM))
    p = jax.nn.softmax(s, axis=-1).astype(vh.dtype)
    o = jnp.einsum('bhsm,bmhd->bshd', p, vh)
    return o.reshape(B, S, XATTN_WIDTH)


def setup_inputs(seed: int = 0) -> dict:
    key = jax.random.key(seed)
    ks = jax.random.split(key, 16)
    f32 = jnp.float32
    nrm = lambda k, shape, s: jax.random.normal(k, shape, f32) * s
    return {
        "x": nrm(ks[0], (BATCH, SEQ, D_MODEL), 1.0),
        "mem": nrm(ks[1], (BATCH, MEM_LEN, D_MODEL), 1.0),
        "norm_pre": 1.0 + nrm(ks[2], (DEPTH, D_MODEL), 0.05),
        "w_in": nrm(ks[3], (DEPTH, D_MODEL, IN_WIDTH), D_MODEL ** -0.5),
        "pool_w": nrm(ks[4], (DEPTH, POOL_GROUPS, POOL_GROUP_DIM, POOL_GROUP_DIM), POOL_GROUP_DIM ** -0.5),
        "pool_scale": 1.0 + nrm(ks[5], (DEPTH, POOL_WIDTH), 0.1),
        "sgu_ln_g": 1.0 + nrm(ks[6], (DEPTH, SGU_WIDTH), 0.05),
        "sgu_ln_b": nrm(ks[7], (DEPTH, SGU_WIDTH), 0.02),
        "sgu_w": nrm(ks[8], (DEPTH, SGU_HEADS, SGU_CHUNK, SGU_CHUNK), SGU_CHUNK ** -0.5),
        "sgu_b": 1.0 + nrm(ks[9], (DEPTH, SGU_HEADS, SGU_CHUNK), 0.1),
        "mem_norm": 1.0 + nrm(ks[10], (D_MODEL,), 0.05),
        "w_kv": nrm(ks[11], (DEPTH, D_MODEL, 2 * XATTN_WIDTH), D_MODEL ** -0.5),
        "branch_norm": 1.0 + nrm(ks[12], (DEPTH, MIX_WIDTH), 0.05),
        "w_out": nrm(ks[13], (DEPTH, MIX_WIDTH, D_MODEL), MIX_WIDTH ** -0.5),
        "norm_post": 1.0 + nrm(ks[14], (DEPTH, D_MODEL), 0.05),
    }


def reference(x, mem, norm_pre, w_in, pool_w, pool_scale, sgu_ln_g, sgu_ln_b, sgu_w, sgu_b,
              mem_norm, w_kv, branch_norm, w_out, norm_post):
    mem_n = rmsnorm(mem, mem_norm)
    for l in range(DEPTH):
        h = rmsnorm(x, norm_pre[l])
        proj = jnp.einsum('bsd,de->bse', h, w_in[l])
        xa, ga, u, vb, gb, q, gc = split_cols(proj, IN_SPLITS)
        k_m, v_m = split_cols(jnp.einsum('bmd,de->bme', mem_n, w_kv[l]), (XATTN_WIDTH, XATTN_WIDTH))

        ya = pool_mixer(xa, pool_w[l], pool_scale[l]) * jax.nn.silu(ga)
        yb = spatial_gating(u, vb, sgu_ln_g[l], sgu_ln_b[l], sgu_w[l], sgu_b[l]) * jax.nn.silu(gb)
        yc = memory_cross_attention(q, k_m, v_m) * jax.nn.silu(gc)

        g_a, g_b, g_c = split_cols(branch_norm[l], BRANCH_WIDTHS)
        y = jnp.concatenate([rmsnorm(ya, g_a), rmsnorm(yb, g_b), rmsnorm(yc, g_c)], axis=-1)
        out = jnp.einsum('bse,ed->bsd', y, w_out[l])
        x = x + rmsnorm(out, norm_post[l])
    return x
```

```python
import functools
import math

import jax
import jax.numpy as jnp
from jax import lax
from jax.experimental import pallas as pl
from jax.experimental.pallas import tpu as pltpu

EPS = 1e-6
POOL_WINDOWS = (2, 4, 8, 16)
POOL_HALO = 16
SGU_CHUNK = 128
SGU_HEADS = 8
XATTN_HEADS = 4

MIXER_TILE = 256
OUT_TILE = 512
MIB = 1024 * 1024
MIXER_VMEM_BYTES = 56 * MIB
OUT_VMEM_BYTES = 48 * MIB
KV_VMEM_BYTES = 40 * MIB

F32 = jnp.float32
BF16 = jnp.bfloat16


def _rms_scale(v):
    return v * lax.rsqrt(jnp.mean(v * v, axis=-1, keepdims=True) + EPS)


def _silu(v):
    return v / (1.0 + jnp.exp(-v))


def _dot(a, b):
    return jnp.dot(a, b, preferred_element_type=F32)


def _memory_kv_kernel(mem_ref, g_ref, wkv_ref, kt_ref, v_ref):
    width = v_ref.shape[-1]
    mem_n = (_rms_scale(mem_ref[0]) * g_ref[...]).astype(BF16)
    kv = _dot(mem_n, wkv_ref[...])
    kt_ref[0] = kv[:, :width].T.astype(BF16)
    v_ref[0] = kv[:, width:].astype(BF16)


def _memory_kv(mem, mem_norm, w_kv):
    batch, mem_len, d_model = mem.shape
    width = w_kv.shape[1] // 2
    resident = pl.BlockSpec(memory_space=pltpu.VMEM)
    return pl.pallas_call(
        _memory_kv_kernel,
        grid=(batch,),
        in_specs=[
            pl.BlockSpec((1, mem_len, d_model), lambda b: (b, 0, 0)),
            resident,
            resident,
        ],
        out_specs=[
            pl.BlockSpec((1, width, mem_len), lambda b: (b, 0, 0)),
            pl.BlockSpec((1, mem_len, width), lambda b: (b, 0, 0)),
        ],
        out_shape=[
            jax.ShapeDtypeStruct((batch, width, mem_len), BF16),
            jax.ShapeDtypeStruct((batch, mem_len, width), BF16),
        ],
        compiler_params=pltpu.CompilerParams(
            dimension_semantics=("arbitrary",), vmem_limit_bytes=KV_VMEM_BYTES),
        name="memory_kv",
    )(mem, mem_norm.reshape(1, d_model), w_kv)


def _mixers_kernel(x_ref, g_pre_ref, w_in_ref, pool_w_ref, pool_scale_ref, ln_g_ref, ln_b_ref,
                   sgu_w_ref, sgu_bt_ref, kt_ref, v_ref, g_br_ref, y_ref, halo_ref):
    tile = x_ref.shape[1]
    width = pool_scale_ref.shape[-1]
    seq_tile = pl.program_id(1)

    h = (_rms_scale(x_ref[0]) * g_pre_ref[...]).astype(BF16)

    @pl.when(seq_tile == 0)
    def _():
        halo_ref[...] = jnp.zeros_like(halo_ref)

    pa = _dot(h, w_in_ref[:, 0:2 * width])
    xa, ga = pa[:, :width], pa[:, width:]
    rows = jnp.concatenate([halo_ref[...], xa], axis=0)
    halo_ref[...] = xa[tile - POOL_HALO:, :]
    seen = (seq_tile * tile + 1 + lax.broadcasted_iota(jnp.int32, (tile, 1), 0)).astype(F32)
    group_dim = width // len(POOL_WINDOWS)
    ya_groups = []
    for g, window in enumerate(POOL_WINDOWS):
        s = rows[:, g * group_dim:(g + 1) * group_dim]
        shift = 1
        while shift < window:
            s = s + pltpu.roll(s, shift, axis=0)
            shift *= 2
        xg = xa[:, g * group_dim:(g + 1) * group_dim]
        d = s[POOL_HALO:, :] / jnp.minimum(seen, float(window)) - xg
        ya_groups.append(_dot(d.astype(BF16), pool_w_ref[g]))
    ya = jnp.concatenate(ya_groups, axis=-1) * pool_scale_ref[...] * _silu(ga)
    y_ref[0, :, 0:width] = (_rms_scale(ya) * g_br_ref[:, 0:width]).astype(BF16)

    pb = _dot(h, w_in_ref[:, 2 * width:5 * width])
    u, vb, gb = pb[:, :width], pb[:, width:2 * width], pb[:, 2 * width:]
    vc = vb - jnp.mean(vb, axis=-1, keepdims=True)
    vn = vc * lax.rsqrt(jnp.mean(vc * vc, axis=-1, keepdims=True) + EPS)
    vn = (vn * ln_g_ref[...] + ln_b_ref[...]).astype(BF16)
    causal = (lax.broadcasted_iota(jnp.int32, (SGU_CHUNK, SGU_CHUNK), 0)
              >= lax.broadcasted_iota(jnp.int32, (SGU_CHUNK, SGU_CHUNK), 1))
    head_dim = width // SGU_HEADS
    z_heads = []
    for hd in range(SGU_HEADS):
        w_hd = jnp.where(causal, sgu_w_ref[hd], 0.0).astype(BF16)
        bias = sgu_bt_ref[:, hd:hd + 1]
        z_chunks = []
        for c in range(tile // SGU_CHUNK):
            v_blk = vn[c * SGU_CHUNK:(c + 1) * SGU_CHUNK, hd * head_dim:(hd + 1) * head_dim]
            z_chunks.append(_dot(w_hd, v_blk) + bias)
        z_heads.append(jnp.concatenate(z_chunks, axis=0))
    yb = u * jnp.concatenate(z_heads, axis=-1) * _silu(gb)
    y_ref[0, :, width:2 * width] = (_rms_scale(yb) * g_br_ref[:, width:2 * width]).astype(BF16)

    pc = _dot(h, w_in_ref[:, 5 * width:7 * width])
    q, gc = pc[:, :width], pc[:, width:]
    attn_dim = width // XATTN_HEADS
    o_heads = []
    for hd in range(XATTN_HEADS):
        cols = slice(hd * attn_dim, (hd + 1) * attn_dim)
        s = _dot(q[:, cols].astype(BF16), kt_ref[0, cols, :]) * (1.0 / math.sqrt(attn_dim))
        e = jnp.exp(s - jnp.max(s, axis=-1, keepdims=True))
        p = e / jnp.sum(e, axis=-1, keepdims=True)
        o_heads.append(_dot(p.astype(BF16), v_ref[0, :, cols]))
    yc = jnp.concatenate(o_heads, axis=-1) * _silu(gc)
    y_ref[0, :, 2 * width:3 * width] = (
        _rms_scale(yc) * g_br_ref[:, 2 * width:3 * width]).astype(BF16)


def _mixers(x, g_pre, w_in, pool_w, pool_scale, ln_g, ln_b, sgu_w, sgu_bt, kt, v, g_branch):
    batch, seq, d_model = x.shape
    width = pool_scale.shape[-1]
    mem_len = v.shape[1]
    resident = pl.BlockSpec(memory_space=pltpu.VMEM)
    return pl.pallas_call(
        _mixers_kernel,
        grid=(batch, seq // MIXER_TILE),
        in_specs=[
            pl.BlockSpec((1, MIXER_TILE, d_model), lambda b, j: (b, j, 0)),
            resident,
            resident,
            resident,
            resident,
            resident,
            resident,
            resident,
            resident,
            pl.BlockSpec((1, width, mem_len), lambda b, j: (b, 0, 0)),
            pl.BlockSpec((1, mem_len, width), lambda b, j: (b, 0, 0)),
            resident,
        ],
        out_specs=pl.BlockSpec((1, MIXER_TILE, 3 * width), lambda b, j: (b, j, 0)),
        out_shape=jax.ShapeDtypeStruct((batch, seq, 3 * width), BF16),
        scratch_shapes=[pltpu.VMEM((POOL_HALO, width), F32)],
        compiler_params=pltpu.CompilerParams(
            dimension_semantics=("arbitrary", "arbitrary"), vmem_limit_bytes=MIXER_VMEM_BYTES),
        name="mixers",
    )(x, g_pre, w_in, pool_w, pool_scale, ln_g, ln_b, sgu_w, sgu_bt, kt, v, g_branch)


def _out_project_kernel(y_ref, x_ref, w_out_ref, g_post_ref, o_ref):
    out = _dot(y_ref[...], w_out_ref[...])
    o_ref[...] = x_ref[...] + _rms_scale(out) * g_post_ref[...]


def _out_project(y, x, w_out, g_post):
    tokens, d_model = x.shape
    mix_width = y.shape[-1]
    resident = pl.BlockSpec(memory_space=pltpu.VMEM)
    return pl.pallas_call(
        _out_project_kernel,
        grid=(tokens // OUT_TILE,),
        in_specs=[
            pl.BlockSpec((OUT_TILE, mix_width), lambda i: (i, 0)),
            pl.BlockSpec((OUT_TILE, d_model), lambda i: (i, 0)),
            resident,
            resident,
        ],
        out_specs=pl.BlockSpec((OUT_TILE, d_model), lambda i: (i, 0)),
        out_shape=jax.ShapeDtypeStruct((tokens, d_model), F32),
        compiler_params=pltpu.CompilerParams(
            dimension_semantics=("arbitrary",), vmem_limit_bytes=OUT_VMEM_BYTES),
        name="out_project",
    )(y, x, w_out, g_post)


def kernel(x, mem, norm_pre, w_in, pool_w, pool_scale, sgu_ln_g, sgu_ln_b, sgu_w, sgu_b,
           mem_norm, w_kv, branch_norm, w_out, norm_post):
    batch, seq, d_model = x.shape
    depth = w_in.shape[0]
    assert seq % MIXER_TILE == 0 and MIXER_TILE % SGU_CHUNK == 0 and MIXER_TILE >= POOL_HALO
    assert (batch * seq) % OUT_TILE == 0
    row = lambda a: a.reshape(1, -1)
    for l in range(depth):
        kt, v = _memory_kv(mem, mem_norm, w_kv[l].astype(BF16))
        y = _mixers(x, row(norm_pre[l]), w_in[l].astype(BF16), pool_w[l].astype(BF16),
                    row(pool_scale[l]), row(sgu_ln_g[l]), row(sgu_ln_b[l]), sgu_w[l],
                    jnp.transpose(sgu_b[l]), kt, v, row(branch_norm[l]))
        x = _out_project(y.reshape(batch * seq, -1), x.reshape(batch * seq, d_model),
                         w_out[l].astype(BF16), row(norm_post[l])).reshape(batch, seq, d_model)
    return x
```

```python
import math

import jax
import jax.numpy as jnp
from jax import lax
from jax.experimental import pallas as pl
from jax.experimental.pallas import tpu as pltpu

EPS = 1e-6
POOL_WINDOWS = (2, 4, 8, 16)
POOL_HALO = 16
SGU_CHUNK = 128
SGU_HEADS = 8
XATTN_HEADS = 4

MIXER_TILE = 512
MIXER_ROWS = 256
OUT_TILE = 512
MIB = 1024 * 1024
MIXER_VMEM_BYTES = 60 * MIB
OUT_VMEM_BYTES = 48 * MIB
KV_VMEM_BYTES = 40 * MIB

F32 = jnp.float32
BF16 = jnp.bfloat16


def _rms_scale(v):
    return v * lax.rsqrt(jnp.mean(v * v, axis=-1, keepdims=True) + EPS)


def _silu(v):
    return v / (1.0 + jnp.exp(-v))


def _dot(a, b):
    return jnp.dot(a, b, preferred_element_type=F32)


def _memory_kv_kernel(mem_ref, g_ref, wkv_ref, kt_ref, v_ref):
    width = v_ref.shape[-1]
    mem_n = (_rms_scale(mem_ref[0]) * g_ref[...]).astype(BF16)
    kv = _dot(mem_n, wkv_ref[...])
    kt_ref[0] = kv[:, :width].T.astype(BF16)
    v_ref[0] = kv[:, width:].astype(BF16)


def _memory_kv(mem, mem_norm, w_kv):
    batch, mem_len, d_model = mem.shape
    width = w_kv.shape[1] // 2
    resident = pl.BlockSpec(memory_space=pltpu.VMEM)
    return pl.pallas_call(
        _memory_kv_kernel,
        grid=(batch,),
        in_specs=[
            pl.BlockSpec((1, mem_len, d_model), lambda b: (b, 0, 0)),
            resident,
            resident,
        ],
        out_specs=[
            pl.BlockSpec((1, width, mem_len), lambda b: (b, 0, 0)),
            pl.BlockSpec((1, mem_len, width), lambda b: (b, 0, 0)),
        ],
        out_shape=[
            jax.ShapeDtypeStruct((batch, width, mem_len), BF16),
            jax.ShapeDtypeStruct((batch, mem_len, width), BF16),
        ],
        compiler_params=pltpu.CompilerParams(
            dimension_semantics=("arbitrary",), vmem_limit_bytes=KV_VMEM_BYTES),
        name="memory_kv",
    )(mem, mem_norm.reshape(1, d_model), w_kv)


def _mix_rows(h, prev_rows, first_pos, w_in_ref, pool_w_ref, pool_scale_ref, ln_g_ref, ln_b_ref,
              sgu_w_ref, sgu_bt_ref, kt_ref, v_ref, g_br_ref, y_ref, row0):
    rows_n = h.shape[0]
    width = pool_scale_ref.shape[-1]
    out_rows = pl.ds(row0, rows_n)

    pa = _dot(h, w_in_ref[:, 0:2 * width])
    xa, ga = pa[:, :width], pa[:, width:]
    rows = jnp.concatenate([prev_rows, xa], axis=0)
    seen = (first_pos + 1 + lax.broadcasted_iota(jnp.int32, (rows_n, 1), 0)).astype(F32)
    group_dim = width // len(POOL_WINDOWS)
    ya_groups = []
    for g, window in enumerate(POOL_WINDOWS):
        s = rows[:, g * group_dim:(g + 1) * group_dim]
        shift = 1
        while shift < window:
            s = s + pltpu.roll(s, shift, axis=0)
            shift *= 2
        xg = xa[:, g * group_dim:(g + 1) * group_dim]
        d = s[POOL_HALO:, :] / jnp.minimum(seen, float(window)) - xg
        ya_groups.append(_dot(d.astype(BF16), pool_w_ref[g]))
    ya = jnp.concatenate(ya_groups, axis=-1) * pool_scale_ref[...] * _silu(ga)
    y_ref[0, out_rows, 0:width] = (_rms_scale(ya) * g_br_ref[:, 0:width]).astype(BF16)

    pb = _dot(h, w_in_ref[:, 2 * width:5 * width])
    u, vb, gb = pb[:, :width], pb[:, width:2 * width], pb[:, 2 * width:]
    vc = vb - jnp.mean(vb, axis=-1, keepdims=True)
    vn = vc * lax.rsqrt(jnp.mean(vc * vc, axis=-1, keepdims=True) + EPS)
    vn = (vn * ln_g_ref[...] + ln_b_ref[...]).astype(BF16)
    causal = (lax.broadcasted_iota(jnp.int32, (SGU_CHUNK, SGU_CHUNK), 0)
              >= lax.broadcasted_iota(jnp.int32, (SGU_CHUNK, SGU_CHUNK), 1))
    head_dim = width // SGU_HEADS
    z_heads = []
    for hd in range(SGU_HEADS):
        w_hd = jnp.where(causal, sgu_w_ref[hd], 0.0).astype(BF16)
        bias = sgu_bt_ref[:, hd:hd + 1]
        z_chunks = []
        for c in range(rows_n // SGU_CHUNK):
            v_blk = vn[c * SGU_CHUNK:(c + 1) * SGU_CHUNK, hd * head_dim:(hd + 1) * head_dim]
            z_chunks.append(_dot(w_hd, v_blk) + bias)
        z_heads.append(jnp.concatenate(z_chunks, axis=0))
    yb = u * jnp.concatenate(z_heads, axis=-1) * _silu(gb)
    y_ref[0, out_rows, width:2 * width] = (
        _rms_scale(yb) * g_br_ref[:, width:2 * width]).astype(BF16)

    pc = _dot(h, w_in_ref[:, 5 * width:7 * width])
    q, gc = pc[:, :width], pc[:, width:]
    attn_dim = width // XATTN_HEADS
    o_heads = []
    for hd in range(XATTN_HEADS):
        cols = slice(hd * attn_dim, (hd + 1) * attn_dim)
        s = _dot(q[:, cols].astype(BF16), kt_ref[0, cols, :]) * (1.0 / math.sqrt(attn_dim))
        e = jnp.exp(s - jnp.max(s, axis=-1, keepdims=True))
        p = e / jnp.sum(e, axis=-1, keepdims=True)
        o_heads.append(_dot(p.astype(BF16), v_ref[0, :, cols]))
    yc = jnp.concatenate(o_heads, axis=-1) * _silu(gc)
    y_ref[0, out_rows, 2 * width:3 * width] = (
        _rms_scale(yc) * g_br_ref[:, 2 * width:3 * width]).astype(BF16)
    return xa[rows_n - POOL_HALO:, :]


def _mixers_kernel(x_ref, g_pre_ref, w_in_ref, pool_w_ref, pool_scale_ref, ln_g_ref, ln_b_ref,
                   sgu_w_ref, sgu_bt_ref, kt_ref, v_ref, g_br_ref, y_ref, halo_ref):
    tile = x_ref.shape[1]
    seq_tile = pl.program_id(1)

    @pl.when(seq_tile == 0)
    def _():
        halo_ref[...] = jnp.zeros_like(halo_ref)

    prev_rows = halo_ref[...]
    for row0 in range(0, tile, MIXER_ROWS):
        h = (_rms_scale(x_ref[0, row0:row0 + MIXER_ROWS, :]) * g_pre_ref[...]).astype(BF16)
        prev_rows = _mix_rows(h, prev_rows, seq_tile * tile + row0, w_in_ref, pool_w_ref,
                              pool_scale_ref, ln_g_ref, ln_b_ref, sgu_w_ref, sgu_bt_ref, kt_ref,
                              v_ref, g_br_ref, y_ref, row0)
    halo_ref[...] = prev_rows


def _mixers(x, g_pre, w_in, pool_w, pool_scale, ln_g, ln_b, sgu_w, sgu_bt, kt, v, g_branch):
    batch, seq, d_model = x.shape
    width = pool_scale.shape[-1]
    mem_len = v.shape[1]
    resident = pl.BlockSpec(memory_space=pltpu.VMEM)
    return pl.pallas_call(
        _mixers_kernel,
        grid=(batch, seq // MIXER_TILE),
        in_specs=[
            pl.BlockSpec((1, MIXER_TILE, d_model), lambda b, j: (b, j, 0)),
            resident,
            resident,
            resident,
            resident,
            resident,
            resident,
            resident,
            resident,
            pl.BlockSpec((1, width, mem_len), lambda b, j: (b, 0, 0)),
            pl.BlockSpec((1, mem_len, width), lambda b, j: (b, 0, 0)),
            resident,
        ],
        out_specs=pl.BlockSpec((1, MIXER_TILE, 3 * width), lambda b, j: (b, j, 0)),
        out_shape=jax.ShapeDtypeStruct((batch, seq, 3 * width), BF16),
        scratch_shapes=[pltpu.VMEM((POOL_HALO, width), F32)],
        compiler_params=pltpu.CompilerParams(
            dimension_semantics=("arbitrary", "arbitrary"), vmem_limit_bytes=MIXER_VMEM_BYTES),
        name="mixers",
    )(x, g_pre, w_in, pool_w, pool_scale, ln_g, ln_b, sgu_w, sgu_bt, kt, v, g_branch)


def _out_project_kernel(y_ref, x_ref, w_out_ref, g_post_ref, o_ref):
    out = _dot(y_ref[...], w_out_ref[...])
    o_ref[...] = x_ref[...] + _rms_scale(out) * g_post_ref[...]


def _out_project(y, x, w_out, g_post):
    tokens, d_model = x.shape
    mix_width = y.shape[-1]
    resident = pl.BlockSpec(memory_space=pltpu.VMEM)
    return pl.pallas_call(
        _out_project_kernel,
        grid=(tokens // OUT_TILE,),
        in_specs=[
            pl.BlockSpec((OUT_TILE, mix_width), lambda i: (i, 0)),
            pl.BlockSpec((OUT_TILE, d_model), lambda i: (i, 0)),
            resident,
            resident,
        ],
        out_specs=pl.BlockSpec((OUT_TILE, d_model), lambda i: (i, 0)),
        out_shape=jax.ShapeDtypeStruct((tokens, d_model), F32),
        compiler_params=pltpu.CompilerParams(
            dimension_semantics=("arbitrary",), vmem_limit_bytes=OUT_VMEM_BYTES),
        name="out_project",
    )(y, x, w_out, g_post)


def kernel(x, mem, norm_pre, w_in, pool_w, pool_scale, sgu_ln_g, sgu_ln_b, sgu_w, sgu_b,
           mem_norm, w_kv, branch_norm, w_out, norm_post):
    batch, seq, d_model = x.shape
    depth = w_in.shape[0]
    assert seq % MIXER_TILE == 0 and MIXER_TILE % MIXER_ROWS == 0
    assert MIXER_ROWS % SGU_CHUNK == 0 and MIXER_ROWS >= POOL_HALO
    assert (batch * seq) % OUT_TILE == 0
    row = lambda a: a.reshape(1, -1)
    for l in range(depth):
        kt, v = _memory_kv(mem, mem_norm, w_kv[l].astype(BF16))
        y = _mixers(x, row(norm_pre[l]), w_in[l].astype(BF16), pool_w[l].astype(BF16),
                    row(pool_scale[l]), row(sgu_ln_g[l]), row(sgu_ln_b[l]), sgu_w[l],
                    jnp.transpose(sgu_b[l]), kt, v, row(branch_norm[l]))
        x = _out_project(y.reshape(batch * seq, -1), x.reshape(batch * seq, d_model),
                         w_out[l].astype(BF16), row(norm_post[l])).reshape(batch, seq, d_model)
    return x
```

```python
import math

import jax
import jax.numpy as jnp
from jax import lax
from jax.experimental import pallas as pl
from jax.experimental.pallas import tpu as pltpu

EPS = 1e-6
POOL_WINDOWS = (2, 4, 8, 16)
POOL_HALO = 16
SGU_CHUNK = 128
SGU_HEADS = 8
XATTN_HEADS = 4

MIXER_TILE = 512
MIXER_ROWS = 256
OUT_TILE = 512
MIB = 1024 * 1024
MIXER_VMEM_BYTES = 60 * MIB
OUT_VMEM_BYTES = 48 * MIB
KV_VMEM_BYTES = 40 * MIB

STAGE_SLOTS = 4
STAGE_BYTES = 512 * 1024
BF16_SUBLANES = 16

F32 = jnp.float32
BF16 = jnp.bfloat16


def _rms_scale(v):
    return v * lax.rsqrt(jnp.mean(v * v, axis=-1, keepdims=True) + EPS)


def _silu(v):
    return v / (1.0 + jnp.exp(-v))


def _dot(a, b):
    return jnp.dot(a, b, preferred_element_type=F32)


def _stage_scratch(n_rows, n_cols):
    rows = max(BF16_SUBLANES, STAGE_BYTES // (4 * n_cols) // BF16_SUBLANES * BF16_SUBLANES)
    assert n_rows % rows == 0
    return [pltpu.VMEM((n_rows, n_cols), BF16),
            pltpu.VMEM((STAGE_SLOTS, rows, n_cols), F32),
            pltpu.SemaphoreType.DMA((STAGE_SLOTS,))]


def _load_weight_bf16(w_hbm, w_vmem, stage, sem):
    rows = stage.shape[1]
    n_chunks = w_hbm.shape[0] // rows

    def chunk_copy(i, slot):
        return pltpu.make_async_copy(w_hbm.at[pl.ds(i * rows, rows), :], stage.at[slot], sem.at[slot])

    for i in range(min(STAGE_SLOTS - 1, n_chunks)):
        chunk_copy(i, i).start()

    def body(i, carry):
        ahead = i + STAGE_SLOTS - 1

        @pl.when(ahead < n_chunks)
        def _():
            chunk_copy(ahead, ahead % STAGE_SLOTS).start()

        slot = i % STAGE_SLOTS
        chunk_copy(i, slot).wait()
        w_vmem[pl.ds(pl.multiple_of(i * rows, rows), rows), :] = stage[slot].astype(BF16)
        return carry

    lax.fori_loop(0, n_chunks, body, 0)


def _memory_kv_kernel(mem_ref, g_ref, wkv_hbm, kt_ref, v_ref, wkv_ref, stage_ref, sem):
    @pl.when(pl.program_id(0) == 0)
    def _():
        _load_weight_bf16(wkv_hbm, wkv_ref, stage_ref, sem)

    width = v_ref.shape[-1]
    mem_n = (_rms_scale(mem_ref[0]) * g_ref[...]).astype(BF16)
    kv = _dot(mem_n, wkv_ref[...])
    kt_ref[0] = kv[:, :width].T.astype(BF16)
    v_ref[0] = kv[:, width:].astype(BF16)


def _memory_kv(mem, mem_norm, w_kv):
    batch, mem_len, d_model = mem.shape
    width = w_kv.shape[1] // 2
    resident = pl.BlockSpec(memory_space=pltpu.VMEM)
    return pl.pallas_call(
        _memory_kv_kernel,
        grid=(batch,),
        in_specs=[
            pl.BlockSpec((1, mem_len, d_model), lambda b: (b, 0, 0)),
            resident,
            pl.BlockSpec(memory_space=pl.ANY),
        ],
        out_specs=[
            pl.BlockSpec((1, width, mem_len), lambda b: (b, 0, 0)),
            pl.BlockSpec((1, mem_len, width), lambda b: (b, 0, 0)),
        ],
        out_shape=[
            jax.ShapeDtypeStruct((batch, width, mem_len), BF16),
            jax.ShapeDtypeStruct((batch, mem_len, width), BF16),
        ],
        scratch_shapes=_stage_scratch(*w_kv.shape),
        compiler_params=pltpu.CompilerParams(
            dimension_semantics=("arbitrary",), vmem_limit_bytes=KV_VMEM_BYTES),
        name="memory_kv",
    )(mem, mem_norm.reshape(1, d_model), w_kv)


def _mix_rows(h, prev_rows, first_pos, w_in_ref, pool_w_ref, pool_scale_ref, ln_g_ref, ln_b_ref,
              sgu_w_ref, sgu_bt_ref, kt_ref, v_ref, g_br_ref, y_ref, row0):
    rows_n = h.shape[0]
    width = pool_scale_ref.shape[-1]
    out_rows = pl.ds(row0, rows_n)

    pa = _dot(h, w_in_ref[:, 0:2 * width])
    xa, ga = pa[:, :width], pa[:, width:]
    rows = jnp.concatenate([prev_rows, xa], axis=0)
    seen = (first_pos + 1 + lax.broadcasted_iota(jnp.int32, (rows_n, 1), 0)).astype(F32)
    group_dim = width // len(POOL_WINDOWS)
    ya_groups = []
    for g, window in enumerate(POOL_WINDOWS):
        s = rows[:, g * group_dim:(g + 1) * group_dim]
        shift = 1
        while shift < window:
            s = s + pltpu.roll(s, shift, axis=0)
            shift *= 2
        xg = xa[:, g * group_dim:(g + 1) * group_dim]
        d = s[POOL_HALO:, :] / jnp.minimum(seen, float(window)) - xg
        ya_groups.append(_dot(d.astype(BF16), pool_w_ref[g].astype(BF16)))
    ya = jnp.concatenate(ya_groups, axis=-1) * pool_scale_ref[...] * _silu(ga)
    y_ref[0, out_rows, 0:width] = (_rms_scale(ya) * g_br_ref[:, 0:width]).astype(BF16)

    pb = _dot(h, w_in_ref[:, 2 * width:5 * width])
    u, vb, gb = pb[:, :width], pb[:, width:2 * width], pb[:, 2 * width:]
    vc = vb - jnp.mean(vb, axis=-1, keepdims=True)
    vn = vc * lax.rsqrt(jnp.mean(vc * vc, axis=-1, keepdims=True) + EPS)
    vn = (vn * ln_g_ref[...] + ln_b_ref[...]).astype(BF16)
    causal = (lax.broadcasted_iota(jnp.int32, (SGU_CHUNK, SGU_CHUNK), 0)
              >= lax.broadcasted_iota(jnp.int32, (SGU_CHUNK, SGU_CHUNK), 1))
    head_dim = width // SGU_HEADS
    z_heads = []
    for hd in range(SGU_HEADS):
        w_hd = jnp.where(causal, sgu_w_ref[hd], 0.0).astype(BF16)
        bias = sgu_bt_ref[:, hd:hd + 1]
        z_chunks = []
        for c in range(rows_n // SGU_CHUNK):
            v_blk = vn[c * SGU_CHUNK:(c + 1) * SGU_CHUNK, hd * head_dim:(hd + 1) * head_dim]
            z_chunks.append(_dot(w_hd, v_blk) + bias)
        z_heads.append(jnp.concatenate(z_chunks, axis=0))
    yb = u * jnp.concatenate(z_heads, axis=-1) * _silu(gb)
    y_ref[0, out_rows, width:2 * width] = (
        _rms_scale(yb) * g_br_ref[:, width:2 * width]).astype(BF16)

    pc = _dot(h, w_in_ref[:, 5 * width:7 * width])
    q, gc = pc[:, :width], pc[:, width:]
    attn_dim = width // XATTN_HEADS
    o_heads = []
    for hd in range(XATTN_HEADS):
        cols = slice(hd * attn_dim, (hd + 1) * attn_dim)
        s = _dot(q[:, cols].astype(BF16), kt_ref[0, cols, :]) * (1.0 / math.sqrt(attn_dim))
        e = jnp.exp(s - jnp.max(s, axis=-1, keepdims=True))
        p = e / jnp.sum(e, axis=-1, keepdims=True)
        o_heads.append(_dot(p.astype(BF16), v_ref[0, :, cols]))
    yc = jnp.concatenate(o_heads, axis=-1) * _silu(gc)
    y_ref[0, out_rows, 2 * width:3 * width] = (
        _rms_scale(yc) * g_br_ref[:, 2 * width:3 * width]).astype(BF16)
    return xa[rows_n - POOL_HALO:, :]


def _mixers_kernel(x_ref, g_pre_ref, w_in_hbm, pool_w_ref, pool_scale_ref, ln_g_ref, ln_b_ref,
                   sgu_w_ref, sgu_bt_ref, kt_ref, v_ref, g_br_ref, y_ref,
                   halo_ref, w_in_ref, stage_ref, sem):
    tile = x_ref.shape[1]
    seq_tile = pl.program_id(1)

    @pl.when((pl.program_id(0) == 0) & (seq_tile == 0))
    def _():
        _load_weight_bf16(w_in_hbm, w_in_ref, stage_ref, sem)

    @pl.when(seq_tile == 0)
    def _():
        halo_ref[...] = jnp.zeros_like(halo_ref)

    prev_rows = halo_ref[...]
    for row0 in range(0, tile, MIXER_ROWS):
        h = (_rms_scale(x_ref[0, row0:row0 + MIXER_ROWS, :]) * g_pre_ref[...]).astype(BF16)
        prev_rows = _mix_rows(h, prev_rows, seq_tile * tile + row0, w_in_ref, pool_w_ref,
                              pool_scale_ref, ln_g_ref, ln_b_ref, sgu_w_ref, sgu_bt_ref, kt_ref,
                              v_ref, g_br_ref, y_ref, row0)
    halo_ref[...] = prev_rows


def _mixers(x, g_pre, w_in, pool_w, pool_scale, ln_g, ln_b, sgu_w, sgu_bt, kt, v, g_branch):
    batch, seq, d_model = x.shape
    width = pool_scale.shape[-1]
    mem_len = v.shape[1]
    resident = pl.BlockSpec(memory_space=pltpu.VMEM)
    return pl.pallas_call(
        _mixers_kernel,
        grid=(batch, seq // MIXER_TILE),
        in_specs=[
            pl.BlockSpec((1, MIXER_TILE, d_model), lambda b, j: (b, j, 0)),
            resident,
            pl.BlockSpec(memory_space=pl.ANY),
            resident,
            resident,
            resident,
            resident,
            resident,
            resident,
            pl.BlockSpec((1, width, mem_len), lambda b, j: (b, 0, 0)),
            pl.BlockSpec((1, mem_len, width), lambda b, j: (b, 0, 0)),
            resident,
        ],
        out_specs=pl.BlockSpec((1, MIXER_TILE, 3 * width), lambda b, j: (b, j, 0)),
        out_shape=jax.ShapeDtypeStruct((batch, seq, 3 * width), BF16),
        scratch_shapes=[pltpu.VMEM((POOL_HALO, width), F32)] + _stage_scratch(*w_in.shape),
        compiler_params=pltpu.CompilerParams(
            dimension_semantics=("arbitrary", "arbitrary"), vmem_limit_bytes=MIXER_VMEM_BYTES),
        name="mixers",
    )(x, g_pre, w_in, pool_w, pool_scale, ln_g, ln_b, sgu_w, sgu_bt, kt, v, g_branch)


def _out_project_kernel(y_ref, x_ref, w_out_hbm, g_post_ref, o_ref, w_out_ref, stage_ref, sem):
    @pl.when(pl.program_id(0) == 0)
    def _():
        _load_weight_bf16(w_out_hbm, w_out_ref, stage_ref, sem)

    out = _dot(y_ref[...], w_out_ref[...])
    o_ref[...] = x_ref[...] + _rms_scale(out) * g_post_ref[...]


def _out_project(y, x, w_out, g_post):
    tokens, d_model = x.shape
    mix_width = y.shape[-1]
    return pl.pallas_call(
        _out_project_kernel,
        grid=(tokens // OUT_TILE,),
        in_specs=[
            pl.BlockSpec((OUT_TILE, mix_width), lambda i: (i, 0)),
            pl.BlockSpec((OUT_TILE, d_model), lambda i: (i, 0)),
            pl.BlockSpec(memory_space=pl.ANY),
            pl.BlockSpec(memory_space=pltpu.VMEM),
        ],
        out_specs=pl.BlockSpec((OUT_TILE, d_model), lambda i: (i, 0)),
        out_shape=jax.ShapeDtypeStruct((tokens, d_model), F32),
        scratch_shapes=_stage_scratch(*w_out.shape),
        compiler_params=pltpu.CompilerParams(
            dimension_semantics=("arbitrary",), vmem_limit_bytes=OUT_VMEM_BYTES),
        name="out_project",
    )(y, x, w_out, g_post)


def kernel(x, mem, norm_pre, w_in, pool_w, pool_scale, sgu_ln_g, sgu_ln_b, sgu_w, sgu_b,
           mem_norm, w_kv, branch_norm, w_out, norm_post):
    batch, seq, d_model = x.shape
    depth = w_in.shape[0]
    assert seq % MIXER_TILE == 0 and MIXER_TILE % MIXER_ROWS == 0
    assert MIXER_ROWS % SGU_CHUNK == 0 and MIXER_ROWS >= POOL_HALO
    assert (batch * seq) % OUT_TILE == 0
    row = lambda a: a.reshape(1, -1)
    for l in range(depth):
        kt, v = _memory_kv(mem, mem_norm, w_kv[l])
        y = _mixers(x, row(norm_pre[l]), w_in[l], pool_w[l], row(pool_scale[l]),
                    row(sgu_ln_g[l]), row(sgu_ln_b[l]), sgu_w[l], jnp.transpose(sgu_b[l]),
                    kt, v, row(branch_norm[l]))
        x = _out_project(y.reshape(batch * seq, -1), x.reshape(batch * seq, d_model),
                         w_out[l], row(norm_post[l])).reshape(batch, seq, d_model)
    return x
```

```python
import math

import jax
import jax.numpy as jnp
from jax import lax
from jax.experimental import pallas as pl
from jax.experimental.pallas import tpu as pltpu

EPS = 1e-6
POOL_WINDOWS = (2, 4, 8, 16)
POOL_HALO = 16
SGU_CHUNK = 128
SGU_HEADS = 8
XATTN_HEADS = 4

MIXER_TILE = 512
MIXER_ROWS = 256
OUT_TILE = 512
MIB = 1024 * 1024
MIXER_VMEM_BYTES = 62 * MIB
OUT_VMEM_BYTES = 48 * MIB
KV_VMEM_BYTES = 40 * MIB

STAGE_SLOTS = 8
STAGE_BYTES = 512 * 1024
BF16_SUBLANES = 16

F32 = jnp.float32
BF16 = jnp.bfloat16


def _rms_scale(v):
    return v * lax.rsqrt(jnp.mean(v * v, axis=-1, keepdims=True) + EPS)


def _silu(v):
    return v / (1.0 + jnp.exp(-v))


def _dot(a, b):
    return jnp.dot(a, b, preferred_element_type=F32)


def _stage_scratch(n_rows, n_cols):
    rows = max(BF16_SUBLANES, STAGE_BYTES // (4 * n_cols) // BF16_SUBLANES * BF16_SUBLANES)
    assert n_rows % rows == 0
    return [pltpu.VMEM((n_rows, n_cols), BF16),
            pltpu.VMEM((STAGE_SLOTS, rows, n_cols), F32),
            pltpu.SemaphoreType.DMA((STAGE_SLOTS,))]


def _load_weight_bf16(w_hbm, w_vmem, stage, sem):
    rows = stage.shape[1]
    n_chunks = w_hbm.shape[0] // rows

    def chunk_copy(i, slot):
        return pltpu.make_async_copy(w_hbm.at[pl.ds(i * rows, rows), :], stage.at[slot], sem.at[slot])

    for i in range(min(STAGE_SLOTS - 1, n_chunks)):
        chunk_copy(i, i).start()

    def body(i, carry):
        ahead = i + STAGE_SLOTS - 1

        @pl.when(ahead < n_chunks)
        def _():
            chunk_copy(ahead, ahead % STAGE_SLOTS).start()

        slot = i % STAGE_SLOTS
        chunk_copy(i, slot).wait()
        w_vmem[pl.ds(pl.multiple_of(i * rows, rows), rows), :] = stage[slot].astype(BF16)
        return carry

    lax.fori_loop(0, n_chunks, body, 0)


def _memory_kv_kernel(mem_ref, g_ref, wkv_hbm, kt_ref, v_ref, wkv_ref, stage_ref, sem):
    @pl.when(pl.program_id(0) == 0)
    def _():
        _load_weight_bf16(wkv_hbm, wkv_ref, stage_ref, sem)

    width = v_ref.shape[-1]
    mem_n = (_rms_scale(mem_ref[0]) * g_ref[...]).astype(BF16)
    kv = _dot(mem_n, wkv_ref[...])
    kt_ref[0] = kv[:, :width].T.astype(BF16)
    v_ref[0] = kv[:, width:].astype(BF16)


def _memory_kv(mem, mem_norm, w_kv):
    batch, mem_len, d_model = mem.shape
    width = w_kv.shape[1] // 2
    resident = pl.BlockSpec(memory_space=pltpu.VMEM)
    return pl.pallas_call(
        _memory_kv_kernel,
        grid=(batch,),
        in_specs=[
            pl.BlockSpec((1, mem_len, d_model), lambda b: (b, 0, 0)),
            resident,
            pl.BlockSpec(memory_space=pl.ANY),
        ],
        out_specs=[
            pl.BlockSpec((1, width, mem_len), lambda b: (b, 0, 0)),
            pl.BlockSpec((1, mem_len, width), lambda b: (b, 0, 0)),
        ],
        out_shape=[
            jax.ShapeDtypeStruct((batch, width, mem_len), BF16),
            jax.ShapeDtypeStruct((batch, mem_len, width), BF16),
        ],
        scratch_shapes=_stage_scratch(*w_kv.shape),
        compiler_params=pltpu.CompilerParams(
            dimension_semantics=("arbitrary",), vmem_limit_bytes=KV_VMEM_BYTES),
        name="memory_kv",
    )(mem, mem_norm.reshape(1, d_model), w_kv)


def _mix_rows(h, prev_rows, first_pos, w_in_ref, pool_w_ref, pool_scale_ref, ln_g_ref, ln_b_ref,
              sgu_w_ref, sgu_bt_ref, kt_ref, v_ref, g_br_ref, y_ref, row0):
    rows_n = h.shape[0]
    width = pool_scale_ref.shape[-1]
    out_rows = pl.ds(row0, rows_n)

    pa = _dot(h, w_in_ref[:, 0:2 * width])
    xa, ga = pa[:, :width], pa[:, width:]
    rows = jnp.concatenate([prev_rows, xa], axis=0)
    seen = (first_pos + 1 + lax.broadcasted_iota(jnp.int32, (rows_n, 1), 0)).astype(F32)
    group_dim = width // len(POOL_WINDOWS)
    ya_groups = []
    for g, window in enumerate(POOL_WINDOWS):
        s = rows[:, g * group_dim:(g + 1) * group_dim]
        shift = 1
        while shift < window:
            s = s + pltpu.roll(s, shift, axis=0)
            shift *= 2
        xg = xa[:, g * group_dim:(g + 1) * group_dim]
        d = s[POOL_HALO:, :] / jnp.minimum(seen, float(window)) - xg
        ya_groups.append(_dot(d.astype(BF16), pool_w_ref[g].astype(BF16)))
    ya = jnp.concatenate(ya_groups, axis=-1) * pool_scale_ref[...] * _silu(ga)
    y_ref[0, out_rows, 0:width] = (_rms_scale(ya) * g_br_ref[:, 0:width]).astype(BF16)

    pb = _dot(h, w_in_ref[:, 2 * width:5 * width])
    u, vb, gb = pb[:, :width], pb[:, width:2 * width], pb[:, 2 * width:]
    vc = vb - jnp.mean(vb, axis=-1, keepdims=True)
    vn = vc * lax.rsqrt(jnp.mean(vc * vc, axis=-1, keepdims=True) + EPS)
    vn = (vn * ln_g_ref[...] + ln_b_ref[...]).astype(BF16)
    causal = (lax.broadcasted_iota(jnp.int32, (SGU_CHUNK, SGU_CHUNK), 0)
              >= lax.broadcasted_iota(jnp.int32, (SGU_CHUNK, SGU_CHUNK), 1))
    head_dim = width // SGU_HEADS
    z_heads = []
    for hd in range(SGU_HEADS):
        w_hd = jnp.where(causal, sgu_w_ref[hd], 0.0).astype(BF16)
        bias = sgu_bt_ref[:, hd:hd + 1]
        z_chunks = []
        for c in range(rows_n // SGU_CHUNK):
            v_blk = vn[c * SGU_CHUNK:(c + 1) * SGU_CHUNK, hd * head_dim:(hd + 1) * head_dim]
            z_chunks.append(_dot(w_hd, v_blk) + bias)
        z_heads.append(jnp.concatenate(z_chunks, axis=0))
    yb = u * jnp.concatenate(z_heads, axis=-1) * _silu(gb)
    y_ref[0, out_rows, width:2 * width] = (
        _rms_scale(yb) * g_br_ref[:, width:2 * width]).astype(BF16)

    pc = _dot(h, w_in_ref[:, 5 * width:7 * width])
    q, gc = pc[:, :width], pc[:, width:]
    attn_dim = width // XATTN_HEADS
    o_heads = []
    for hd in range(XATTN_HEADS):
        cols = slice(hd * attn_dim, (hd + 1) * attn_dim)
        s = _dot(q[:, cols].astype(BF16), kt_ref[0, cols, :]) * (1.0 / math.sqrt(attn_dim))
        e = jnp.exp(s - jnp.max(s, axis=-1, keepdims=True))
        p = e / jnp.sum(e, axis=-1, keepdims=True)
        o_heads.append(_dot(p.astype(BF16), v_ref[0, :, cols]))
    yc = jnp.concatenate(o_heads, axis=-1) * _silu(gc)
    y_ref[0, out_rows, 2 * width:3 * width] = (
        _rms_scale(yc) * g_br_ref[:, 2 * width:3 * width]).astype(BF16)
    return xa[rows_n - POOL_HALO:, :]


def _mixers_kernel(x_ref, g_pre_ref, w_in_hbm, pool_w_ref, pool_scale_ref, ln_g_ref, ln_b_ref,
                   sgu_w_ref, sgu_bt_ref, kt_ref, v_ref, g_br_ref, y_ref,
                   halo_ref, w_in_ref, stage_ref, sem):
    tile = x_ref.shape[1]
    seq_tile = pl.program_id(1)

    @pl.when((pl.program_id(0) == 0) & (seq_tile == 0))
    def _():
        _load_weight_bf16(w_in_hbm, w_in_ref, stage_ref, sem)

    @pl.when(seq_tile == 0)
    def _():
        halo_ref[...] = jnp.zeros_like(halo_ref)

    prev_rows = halo_ref[...]
    for row0 in range(0, tile, MIXER_ROWS):
        h = (_rms_scale(x_ref[0, row0:row0 + MIXER_ROWS, :]) * g_pre_ref[...]).astype(BF16)
        prev_rows = _mix_rows(h, prev_rows, seq_tile * tile + row0, w_in_ref, pool_w_ref,
                              pool_scale_ref, ln_g_ref, ln_b_ref, sgu_w_ref, sgu_bt_ref, kt_ref,
                              v_ref, g_br_ref, y_ref, row0)
    halo_ref[...] = prev_rows


def _mixers(x, g_pre, w_in, pool_w, pool_scale, ln_g, ln_b, sgu_w, sgu_bt, kt, v, g_branch):
    batch, seq, d_model = x.shape
    width = pool_scale.shape[-1]
    mem_len = v.shape[1]
    resident = pl.BlockSpec(memory_space=pltpu.VMEM)
    return pl.pallas_call(
        _mixers_kernel,
        grid=(batch, seq // MIXER_TILE),
        in_specs=[
            pl.BlockSpec((1, MIXER_TILE, d_model), lambda b, j: (b, j, 0)),
            resident,
            pl.BlockSpec(memory_space=pl.ANY),
            resident,
            resident,
            resident,
            resident,
            resident,
            resident,
            pl.BlockSpec((1, width, mem_len), lambda b, j: (b, 0, 0)),
            pl.BlockSpec((1, mem_len, width), lambda b, j: (b, 0, 0)),
            resident,
        ],
        out_specs=pl.BlockSpec((1, MIXER_TILE, 3 * width), lambda b, j: (b, j, 0)),
        out_shape=jax.ShapeDtypeStruct((batch, seq, 3 * width), BF16),
        scratch_shapes=[pltpu.VMEM((POOL_HALO, width), F32)] + _stage_scratch(*w_in.shape),
        compiler_params=pltpu.CompilerParams(
            dimension_semantics=("arbitrary", "arbitrary"), vmem_limit_bytes=MIXER_VMEM_BYTES),
        name="mixers",
    )(x, g_pre, w_in, pool_w, pool_scale, ln_g, ln_b, sgu_w, sgu_bt, kt, v, g_branch)


def _out_project_kernel(y_ref, x_ref, w_out_hbm, g_post_ref, o_ref, w_out_ref, stage_ref, sem):
    @pl.when(pl.program_id(0) == 0)
    def _():
        _load_weight_bf16(w_out_hbm, w_out_ref, stage_ref, sem)

    out = _dot(y_ref[...], w_out_ref[...])
    o_ref[...] = x_ref[...] + _rms_scale(out) * g_post_ref[...]


def _out_project(y, x, w_out, g_post):
    tokens, d_model = x.shape
    mix_width = y.shape[-1]
    return pl.pallas_call(
        _out_project_kernel,
        grid=(tokens // OUT_TILE,),
        in_specs=[
            pl.BlockSpec((OUT_TILE, mix_width), lambda i: (i, 0)),
            pl.BlockSpec((OUT_TILE, d_model), lambda i: (i, 0)),
            pl.BlockSpec(memory_space=pl.ANY),
            pl.BlockSpec(memory_space=pltpu.VMEM),
        ],
        out_specs=pl.BlockSpec((OUT_TILE, d_model), lambda i: (i, 0)),
        out_shape=jax.ShapeDtypeStruct((tokens, d_model), F32),
        scratch_shapes=_stage_scratch(*w_out.shape),
        compiler_params=pltpu.CompilerParams(
            dimension_semantics=("arbitrary",), vmem_limit_bytes=OUT_VMEM_BYTES),
        name="out_project",
    )(y, x, w_out, g_post)


def kernel(x, mem, norm_pre, w_in, pool_w, pool_scale, sgu_ln_g, sgu_ln_b, sgu_w, sgu_b,
           mem_norm, w_kv, branch_norm, w_out, norm_post):
    batch, seq, d_model = x.shape
    depth = w_in.shape[0]
    assert seq % MIXER_TILE == 0 and MIXER_TILE % MIXER_ROWS == 0
    assert MIXER_ROWS % SGU_CHUNK == 0 and MIXER_ROWS >= POOL_HALO
    assert (batch * seq) % OUT_TILE == 0
    row = lambda a: a.reshape(1, -1)
    for l in range(depth):
        kt, v = _memory_kv(mem, mem_norm, w_kv[l])
        y = _mixers(x, row(norm_pre[l]), w_in[l], pool_w[l], row(pool_scale[l]),
                    row(sgu_ln_g[l]), row(sgu_ln_b[l]), sgu_w[l], jnp.transpose(sgu_b[l]),
                    kt, v, row(branch_norm[l]))
        x = _out_project(y.reshape(batch * seq, -1), x.reshape(batch * seq, d_model),
                         w_out[l], row(norm_post[l])).reshape(batch, seq, d_model)
    return x
```

```python
import math

import jax
import jax.numpy as jnp
from jax import lax
from jax.experimental import pallas as pl
from jax.experimental.pallas import tpu as pltpu

EPS = 1e-6
POOL_WINDOWS = (2, 4, 8, 16)
POOL_HALO = 16
SGU_CHUNK = 128
SGU_HEADS = 8
XATTN_HEADS = 4

MIXER_TILE = 512
MIXER_ROWS = 256
OUT_TILE = 512
MIB = 1024 * 1024
MIXER_VMEM_BYTES = 62 * MIB
OUT_VMEM_BYTES = 48 * MIB
KV_VMEM_BYTES = 40 * MIB

STAGE_SLOTS = 8
STAGE_BYTES = 512 * 1024
BF16_SUBLANES = 16

F32 = jnp.float32
BF16 = jnp.bfloat16


def _rms_scale(v):
    return v * lax.rsqrt(jnp.mean(v * v, axis=-1, keepdims=True) + EPS)


def _silu(v):
    return v / (1.0 + jnp.exp(-v))


def _dot(a, b):
    return jnp.dot(a, b, preferred_element_type=F32)


def _stage_scratch(n_rows, n_cols):
    rows = max(BF16_SUBLANES, STAGE_BYTES // (4 * n_cols) // BF16_SUBLANES * BF16_SUBLANES)
    assert n_rows % rows == 0
    return [pltpu.VMEM((n_rows, n_cols), BF16),
            pltpu.VMEM((STAGE_SLOTS, rows, n_cols), F32),
            pltpu.SemaphoreType.DMA((STAGE_SLOTS,))]


def _load_weight_bf16(w_hbm, w_vmem, stage, sem):
    rows = stage.shape[1]
    n_chunks = w_hbm.shape[0] // rows

    def chunk_copy(i, slot):
        return pltpu.make_async_copy(w_hbm.at[pl.ds(i * rows, rows), :], stage.at[slot], sem.at[slot])

    for i in range(min(STAGE_SLOTS - 1, n_chunks)):
        chunk_copy(i, i).start()

    def body(i, carry):
        ahead = i + STAGE_SLOTS - 1

        @pl.when(ahead < n_chunks)
        def _():
            chunk_copy(ahead, ahead % STAGE_SLOTS).start()

        slot = i % STAGE_SLOTS
        chunk_copy(i, slot).wait()
        w_vmem[pl.ds(pl.multiple_of(i * rows, rows), rows), :] = stage[slot].astype(BF16)
        return carry

    lax.fori_loop(0, n_chunks, body, 0)


def _memory_kv_kernel(mem_ref, g_ref, wkv_hbm, kt_ref, v_ref, wkv_ref, stage_ref, sem):
    @pl.when(pl.program_id(0) == 0)
    def _():
        _load_weight_bf16(wkv_hbm, wkv_ref, stage_ref, sem)

    width = v_ref.shape[-1]
    mem_n = (_rms_scale(mem_ref[0]) * g_ref[...]).astype(BF16)
    kv = _dot(mem_n, wkv_ref[...])
    kt_ref[0] = kv[:, :width].T.astype(BF16)
    v_ref[0] = kv[:, width:].astype(BF16)


def _memory_kv(mem, mem_norm, w_kv):
    batch, mem_len, d_model = mem.shape
    width = w_kv.shape[1] // 2
    resident = pl.BlockSpec(memory_space=pltpu.VMEM)
    return pl.pallas_call(
        _memory_kv_kernel,
        grid=(batch,),
        in_specs=[
            pl.BlockSpec((1, mem_len, d_model), lambda b: (b, 0, 0)),
            resident,
            pl.BlockSpec(memory_space=pl.ANY),
        ],
        out_specs=[
            pl.BlockSpec((1, width, mem_len), lambda b: (b, 0, 0)),
            pl.BlockSpec((1, mem_len, width), lambda b: (b, 0, 0)),
        ],
        out_shape=[
            jax.ShapeDtypeStruct((batch, width, mem_len), BF16),
            jax.ShapeDtypeStruct((batch, mem_len, width), BF16),
        ],
        scratch_shapes=_stage_scratch(*w_kv.shape),
        compiler_params=pltpu.CompilerParams(
            dimension_semantics=("arbitrary",), vmem_limit_bytes=KV_VMEM_BYTES),
        name="memory_kv",
    )(mem, mem_norm.reshape(1, d_model), w_kv)


def _pool_prep(pa, prev_rows, first_pos):
    rows_n = pa.shape[0]
    width = pa.shape[1] // 2
    xa = pa[:, :width]
    rows = jnp.concatenate([prev_rows, xa], axis=0)
    seen = (first_pos + 1 + lax.broadcasted_iota(jnp.int32, (rows_n, 1), 0)).astype(F32)
    group_dim = width // len(POOL_WINDOWS)
    d_groups = []
    for g, window in enumerate(POOL_WINDOWS):
        s = rows[:, g * group_dim:(g + 1) * group_dim]
        shift = 1
        while shift < window:
            s = s + pltpu.roll(s, shift, axis=0)
            shift *= 2
        xg = xa[:, g * group_dim:(g + 1) * group_dim]
        d = s[POOL_HALO:, :] / jnp.minimum(seen, float(window)) - xg
        d_groups.append(d.astype(BF16))
    return d_groups


def _pool_finish(d_groups, pa, pool_w_ref, pool_scale_ref, g_ref):
    ga = pa[:, pa.shape[1] // 2:]
    ya = jnp.concatenate([_dot(d, pool_w_ref[g].astype(BF16)) for g, d in enumerate(d_groups)],
                         axis=-1)
    ya = ya * pool_scale_ref[...] * _silu(ga)
    return (_rms_scale(ya) * g_ref[...]).astype(BF16)


def _sgu_prep(pb, ln_g_ref, ln_b_ref):
    width = pb.shape[1] // 3
    vb = pb[:, width:2 * width]
    vc = vb - jnp.mean(vb, axis=-1, keepdims=True)
    vn = vc * lax.rsqrt(jnp.mean(vc * vc, axis=-1, keepdims=True) + EPS)
    return (vn * ln_g_ref[...] + ln_b_ref[...]).astype(BF16)


def _sgu_finish(vn, pb, sgu_w_ref, sgu_bt_ref, g_ref):
    rows_n = pb.shape[0]
    width = pb.shape[1] // 3
    u, gb = pb[:, :width], pb[:, 2 * width:]
    causal = (lax.broadcasted_iota(jnp.int32, (SGU_CHUNK, SGU_CHUNK), 0)
              >= lax.broadcasted_iota(jnp.int32, (SGU_CHUNK, SGU_CHUNK), 1))
    head_dim = width // SGU_HEADS
    z_heads = []
    for hd in range(SGU_HEADS):
        w_hd = jnp.where(causal, sgu_w_ref[hd], 0.0).astype(BF16)
        bias = sgu_bt_ref[:, hd:hd + 1]
        z_chunks = []
        for c in range(rows_n // SGU_CHUNK):
            v_blk = vn[c * SGU_CHUNK:(c + 1) * SGU_CHUNK, hd * head_dim:(hd + 1) * head_dim]
            z_chunks.append(_dot(w_hd, v_blk) + bias)
        z_heads.append(jnp.concatenate(z_chunks, axis=0))
    yb = u * jnp.concatenate(z_heads, axis=-1) * _silu(gb)
    return (_rms_scale(yb) * g_ref[...]).astype(BF16)


def _attn_prep(pc, kt_ref):
    width = pc.shape[1] // 2
    attn_dim = width // XATTN_HEADS
    p_heads = []
    for hd in range(XATTN_HEADS):
        cols = slice(hd * attn_dim, (hd + 1) * attn_dim)
        s = _dot(pc[:, cols].astype(BF16), kt_ref[0, cols, :]) * (1.0 / math.sqrt(attn_dim))
        e = jnp.exp(s - jnp.max(s, axis=-1, keepdims=True))
        p_heads.append((e / jnp.sum(e, axis=-1, keepdims=True)).astype(BF16))
    return p_heads


def _attn_finish(p_heads, pc, v_ref, g_ref):
    width = pc.shape[1] // 2
    attn_dim = width // XATTN_HEADS
    gc = pc[:, width:]
    o = jnp.concatenate(
        [_dot(p, v_ref[0, :, hd * attn_dim:(hd + 1) * attn_dim]) for hd, p in enumerate(p_heads)],
        axis=-1)
    yc = o * _silu(gc)
    return (_rms_scale(yc) * g_ref[...]).astype(BF16)


def _mixers_kernel(x_ref, g_pre_ref, w_in_hbm, pool_w_ref, pool_scale_ref, ln_g_ref, ln_b_ref,
                   sgu_w_ref, sgu_bt_ref, kt_ref, v_ref, g_br_ref, y_ref,
                   halo_ref, w_in_ref, stage_ref, sem):
    tile = x_ref.shape[1]
    width = pool_scale_ref.shape[-1]
    seq_tile = pl.program_id(1)
    in_cols = {"pool": slice(0, 2 * width),
               "sgu": slice(2 * width, 5 * width),
               "attn": slice(5 * width, 7 * width)}
    out_cols = {"pool": slice(0, width), "sgu": slice(width, 2 * width),
                "attn": slice(2 * width, 3 * width)}

    @pl.when((pl.program_id(0) == 0) & (seq_tile == 0))
    def _():
        _load_weight_bf16(w_in_hbm, w_in_ref, stage_ref, sem)

    @pl.when(seq_tile == 0)
    def _():
        halo_ref[...] = jnp.zeros_like(halo_ref)

    hs, proj, prepped = {}, {}, {}

    def project(r, branch):
        if r not in hs:
            hs[r] = (_rms_scale(x_ref[0, r:r + MIXER_ROWS, :]) * g_pre_ref[...]).astype(BF16)
        proj[r, branch] = _dot(hs[r], w_in_ref[:, in_cols[branch]])

    def prep(r, branch):
        p = proj[r, branch]
        if branch == "pool":
            prev_rows = (halo_ref[...] if r == 0 else
                         proj[r - MIXER_ROWS, "pool"][MIXER_ROWS - POOL_HALO:, :width])
            prepped[r, branch] = _pool_prep(p, prev_rows, seq_tile * tile + r)
        elif branch == "sgu":
            prepped[r, branch] = _sgu_prep(p, ln_g_ref, ln_b_ref)
        else:
            prepped[r, branch] = _attn_prep(p, kt_ref)

    def finish(r, branch):
        p, state, g_ref = proj[r, branch], prepped[r, branch], g_br_ref.at[:, out_cols[branch]]
        if branch == "pool":
            y = _pool_finish(state, p, pool_w_ref, pool_scale_ref, g_ref)
        elif branch == "sgu":
            y = _sgu_finish(state, p, sgu_w_ref, sgu_bt_ref, g_ref)
        else:
            y = _attn_finish(state, p, v_ref, g_ref)
        y_ref[0, pl.ds(r, MIXER_ROWS), out_cols[branch]] = y

    order = [(r, branch) for r in range(0, tile, MIXER_ROWS) for branch in ("pool", "sgu", "attn")]
    for k in range(len(order) + 2):
        if k < len(order):
            project(*order[k])
        if k >= 2:
            finish(*order[k - 2])
        if 1 <= k <= len(order):
            prep(*order[k - 1])
    halo_ref[...] = proj[tile - MIXER_ROWS, "pool"][MIXER_ROWS - POOL_HALO:, :width]


def _mixers(x, g_pre, w_in, pool_w, pool_scale, ln_g, ln_b, sgu_w, sgu_bt, kt, v, g_branch):
    batch, seq, d_model = x.shape
    width = pool_scale.shape[-1]
    mem_len = v.shape[1]
    resident = pl.BlockSpec(memory_space=pltpu.VMEM)
    return pl.pallas_call(
        _mixers_kernel,
        grid=(batch, seq // MIXER_TILE),
        in_specs=[
            pl.BlockSpec((1, MIXER_TILE, d_model), lambda b, j: (b, j, 0)),
            resident,
            pl.BlockSpec(memory_space=pl.ANY),
            resident,
            resident,
            resident,
            resident,
            resident,
            resident,
            pl.BlockSpec((1, width, mem_len), lambda b, j: (b, 0, 0)),
            pl.BlockSpec((1, mem_len, width), lambda b, j: (b, 0, 0)),
            resident,
        ],
        out_specs=pl.BlockSpec((1, MIXER_TILE, 3 * width), lambda b, j: (b, j, 0)),
        out_shape=jax.ShapeDtypeStruct((batch, seq, 3 * width), BF16),
        scratch_shapes=[pltpu.VMEM((POOL_HALO, width), F32)] + _stage_scratch(*w_in.shape),
        compiler_params=pltpu.CompilerParams(
            dimension_semantics=("arbitrary", "arbitrary"), vmem_limit_bytes=MIXER_VMEM_BYTES),
        name="mixers",
    )(x, g_pre, w_in, pool_w, pool_scale, ln_g, ln_b, sgu_w, sgu_bt, kt, v, g_branch)


def _out_project_kernel(y_ref, x_ref, w_out_hbm, g_post_ref, o_ref, w_out_ref, stage_ref, sem):
    @pl.when(pl.program_id(0) == 0)
    def _():
        _load_weight_bf16(w_out_hbm, w_out_ref, stage_ref, sem)

    out = _dot(y_ref[...], w_out_ref[...])
    o_ref[...] = x_ref[...] + _rms_scale(out) * g_post_ref[...]


def _out_project(y, x, w_out, g_post):
    tokens, d_model = x.shape
    mix_width = y.shape[-1]
    return pl.pallas_call(
        _out_project_kernel,
        grid=(tokens // OUT_TILE,),
        in_specs=[
            pl.BlockSpec((OUT_TILE, mix_width), lambda i: (i, 0)),
            pl.BlockSpec((OUT_TILE, d_model), lambda i: (i, 0)),
            pl.BlockSpec(memory_space=pl.ANY),
            pl.BlockSpec(memory_space=pltpu.VMEM),
        ],
        out_specs=pl.BlockSpec((OUT_TILE, d_model), lambda i: (i, 0)),
        out_shape=jax.ShapeDtypeStruct((tokens, d_model), F32),
        scratch_shapes=_stage_scratch(*w_out.shape),
        compiler_params=pltpu.CompilerParams(
            dimension_semantics=("arbitrary",), vmem_limit_bytes=OUT_VMEM_BYTES),
        name="out_project",
    )(y, x, w_out, g_post)


def kernel(x, mem, norm_pre, w_in, pool_w, pool_scale, sgu_ln_g, sgu_ln_b, sgu_w, sgu_b,
           mem_norm, w_kv, branch_norm, w_out, norm_post):
    batch, seq, d_model = x.shape
    depth = w_in.shape[0]
    assert seq % MIXER_TILE == 0 and MIXER_TILE % MIXER_ROWS == 0
    assert MIXER_ROWS % SGU_CHUNK == 0 and MIXER_ROWS >= POOL_HALO
    assert (batch * seq) % OUT_TILE == 0
    row = lambda a: a.reshape(1, -1)
    for l in range(depth):
        kt, v = _memory_kv(mem, mem_norm, w_kv[l])
        y = _mixers(x, row(norm_pre[l]), w_in[l], pool_w[l], row(pool_scale[l]),
                    row(sgu_ln_g[l]), row(sgu_ln_b[l]), sgu_w[l], jnp.transpose(sgu_b[l]),
                    kt, v, row(branch_norm[l]))
        x = _out_project(y.reshape(batch * seq, -1), x.reshape(batch * seq, d_model),
                         w_out[l], row(norm_post[l])).reshape(batch, seq, d_model)
    return x
```

```python
import math

import jax
import jax.numpy as jnp
from jax import lax
from jax.experimental import pallas as pl
from jax.experimental.pallas import tpu as pltpu

EPS = 1e-6
LOG2_E = 1.4426950408889634
POOL_WINDOWS = (2, 4, 8, 16)
POOL_HALO = 16
SGU_CHUNK = 128
SGU_HEADS = 8
XATTN_HEADS = 4

MIXER_TILE = 512
MIXER_ROWS = 256
OUT_TILE = 512
MIB = 1024 * 1024
MIXER_VMEM_BYTES = 62 * MIB
OUT_VMEM_BYTES = 62 * MIB
KV_VMEM_BYTES = 62 * MIB

STAGE_SLOTS = 8
STAGE_BYTES = 512 * 1024
BF16_SUBLANES = 16

F32 = jnp.float32
BF16 = jnp.bfloat16


def _rms_scale(v):
    return v * lax.rsqrt(jnp.mean(v * v, axis=-1, keepdims=True) + EPS)


def _silu(v):
    return v / (1.0 + jnp.exp2(v * -LOG2_E))


def _dot(a, b):
    return jnp.dot(a, b, preferred_element_type=F32)


def _stage_scratch(n_rows, n_cols):
    rows = max(BF16_SUBLANES, STAGE_BYTES // (4 * n_cols) // BF16_SUBLANES * BF16_SUBLANES)
    assert n_rows % rows == 0
    return [pltpu.VMEM((n_rows, n_cols), BF16),
            pltpu.VMEM((STAGE_SLOTS, rows, n_cols), F32),
            pltpu.SemaphoreType.DMA((STAGE_SLOTS,))]


def _load_weight_bf16(w_hbm, w_vmem, stage, sem):
    rows = stage.shape[1]
    n_chunks = w_hbm.shape[0] // rows

    def chunk_copy(i, slot):
        return pltpu.make_async_copy(w_hbm.at[pl.ds(i * rows, rows), :], stage.at[slot], sem.at[slot])

    for i in range(min(STAGE_SLOTS - 1, n_chunks)):
        chunk_copy(i, i).start()

    def body(i, carry):
        ahead = i + STAGE_SLOTS - 1

        @pl.when(ahead < n_chunks)
        def _():
            chunk_copy(ahead, ahead % STAGE_SLOTS).start()

        slot = i % STAGE_SLOTS
        chunk_copy(i, slot).wait()
        w_vmem[pl.ds(pl.multiple_of(i * rows, rows), rows), :] = stage[slot].astype(BF16)
        return carry

    lax.fori_loop(0, n_chunks, body, 0)


def _memory_kv_kernel(mem_ref, g_ref, wkv_hbm, kt_ref, v_ref, wkv_ref, stage_ref, sem):
    @pl.when(pl.program_id(0) == 0)
    def _():
        _load_weight_bf16(wkv_hbm, wkv_ref, stage_ref, sem)

    width = v_ref.shape[-1]
    mem_n = (_rms_scale(mem_ref[0]) * g_ref[...]).astype(BF16)
    kv = _dot(mem_n, wkv_ref[...])
    k = kv[:, :width] * (1.0 / math.sqrt(width // XATTN_HEADS))
    kt_ref[0] = k.T.astype(BF16)
    v_ref[0] = kv[:, width:].astype(BF16)


def _memory_kv(mem, mem_norm, w_kv):
    batch, mem_len, d_model = mem.shape
    width = w_kv.shape[1] // 2
    resident = pl.BlockSpec(memory_space=pltpu.VMEM)
    return pl.pallas_call(
        _memory_kv_kernel,
        grid=(batch,),
        in_specs=[
            pl.BlockSpec((1, mem_len, d_model), lambda b: (b, 0, 0)),
            resident,
            pl.BlockSpec(memory_space=pl.ANY),
        ],
        out_specs=[
            pl.BlockSpec((1, width, mem_len), lambda b: (b, 0, 0)),
            pl.BlockSpec((1, mem_len, width), lambda b: (b, 0, 0)),
        ],
        out_shape=[
            jax.ShapeDtypeStruct((batch, width, mem_len), BF16),
            jax.ShapeDtypeStruct((batch, mem_len, width), BF16),
        ],
        scratch_shapes=_stage_scratch(*w_kv.shape),
        compiler_params=pltpu.CompilerParams(
            dimension_semantics=("arbitrary",), vmem_limit_bytes=KV_VMEM_BYTES),
        name="memory_kv",
    )(mem, mem_norm.reshape(1, d_model), w_kv)


def _mix_rows(h, prev_rows, first_pos, w_in_ref, pool_w_ref, pool_scale_ref, ln_g_ref, ln_b_ref,
              sgu_w_ref, sgu_bt_ref, kt_ref, v_ref, g_br_ref, y_ref, row0):
    rows_n = h.shape[0]
    width = pool_scale_ref.shape[-1]
    out_rows = pl.ds(row0, rows_n)

    pa = _dot(h, w_in_ref[:, 0:2 * width])
    xa, ga = pa[:, :width], pa[:, width:]
    rows = jnp.concatenate([prev_rows, xa], axis=0)
    seen = (first_pos + 1 + lax.broadcasted_iota(jnp.int32, (rows_n, 1), 0)).astype(F32)
    group_dim = width // len(POOL_WINDOWS)
    ya_groups = []
    for g, window in enumerate(POOL_WINDOWS):
        s = rows[:, g * group_dim:(g + 1) * group_dim]
        shift = 1
        while shift < window:
            s = s + pltpu.roll(s, shift, axis=0)
            shift *= 2
        xg = xa[:, g * group_dim:(g + 1) * group_dim]
        d = s[POOL_HALO:, :] / jnp.minimum(seen, float(window)) - xg
        ya_groups.append(_dot(d.astype(BF16), pool_w_ref[g].astype(BF16)))
    ya = jnp.concatenate(ya_groups, axis=-1) * pool_scale_ref[...] * _silu(ga)
    y_ref[0, out_rows, 0:width] = (_rms_scale(ya) * g_br_ref[:, 0:width]).astype(BF16)

    pb = _dot(h, w_in_ref[:, 2 * width:5 * width])
    u, vb, gb = pb[:, :width], pb[:, width:2 * width], pb[:, 2 * width:]
    vc = vb - jnp.mean(vb, axis=-1, keepdims=True)
    vn = vc * lax.rsqrt(jnp.mean(vc * vc, axis=-1, keepdims=True) + EPS)
    vn = (vn * ln_g_ref[...] + ln_b_ref[...]).astype(BF16)
    causal = (lax.broadcasted_iota(jnp.int32, (SGU_CHUNK, SGU_CHUNK), 0)
              >= lax.broadcasted_iota(jnp.int32, (SGU_CHUNK, SGU_CHUNK), 1))
    head_dim = width // SGU_HEADS
    z_heads = []
    for hd in range(SGU_HEADS):
        w_hd = jnp.where(causal, sgu_w_ref[hd], 0.0).astype(BF16)
        bias = sgu_bt_ref[:, hd:hd + 1]
        z_chunks = []
        for c in range(rows_n // SGU_CHUNK):
            v_blk = vn[c * SGU_CHUNK:(c + 1) * SGU_CHUNK, hd * head_dim:(hd + 1) * head_dim]
            z_chunks.append(_dot(w_hd, v_blk) + bias)
        z_heads.append(jnp.concatenate(z_chunks, axis=0))
    yb = u * jnp.concatenate(z_heads, axis=-1) * _silu(gb)
    y_ref[0, out_rows, width:2 * width] = (
        _rms_scale(yb) * g_br_ref[:, width:2 * width]).astype(BF16)

    pc = _dot(h, w_in_ref[:, 5 * width:7 * width])
    q, gc = pc[:, :width], pc[:, width:]
    attn_dim = width // XATTN_HEADS
    o_heads = []
    for hd in range(XATTN_HEADS):
        cols = slice(hd * attn_dim, (hd + 1) * attn_dim)
        s = _dot(q[:, cols].astype(BF16), kt_ref[0, cols, :])
        e = jnp.exp(s - jnp.max(s, axis=-1, keepdims=True))
        p = e / jnp.sum(e, axis=-1, keepdims=True)
        o_heads.append(_dot(p.astype(BF16), v_ref[0, :, cols]))
    yc = jnp.concatenate(o_heads, axis=-1) * _silu(gc)
    y_ref[0, out_rows, 2 * width:3 * width] = (
        _rms_scale(yc) * g_br_ref[:, 2 * width:3 * width]).astype(BF16)
    return xa[rows_n - POOL_HALO:, :]


def _mixers_kernel(x_ref, g_pre_ref, w_in_hbm, pool_w_ref, pool_scale_ref, ln_g_ref, ln_b_ref,
                   sgu_w_ref, sgu_bt_ref, kt_ref, v_ref, g_br_ref, y_ref,
                   halo_ref, w_in_ref, stage_ref, sem):
    tile = x_ref.shape[1]
    seq_tile = pl.program_id(1)

    @pl.when((pl.program_id(0) == 0) & (seq_tile == 0))
    def _():
        _load_weight_bf16(w_in_hbm, w_in_ref, stage_ref, sem)

    @pl.when(seq_tile == 0)
    def _():
        halo_ref[...] = jnp.zeros_like(halo_ref)

    prev_rows = halo_ref[...]
    for row0 in range(0, tile, MIXER_ROWS):
        h = (_rms_scale(x_ref[0, row0:row0 + MIXER_ROWS, :]) * g_pre_ref[...]).astype(BF16)
        prev_rows = _mix_rows(h, prev_rows, seq_tile * tile + row0, w_in_ref, pool_w_ref,
                              pool_scale_ref, ln_g_ref, ln_b_ref, sgu_w_ref, sgu_bt_ref, kt_ref,
                              v_ref, g_br_ref, y_ref, row0)
    halo_ref[...] = prev_rows


def _mixers(x, g_pre, w_in, pool_w, pool_scale, ln_g, ln_b, sgu_w, sgu_bt, kt, v, g_branch):
    batch, seq, d_model = x.shape
    width = pool_scale.shape[-1]
    mem_len = v.shape[1]
    resident = pl.BlockSpec(memory_space=pltpu.VMEM)
    return pl.pallas_call(
        _mixers_kernel,
        grid=(batch, seq // MIXER_TILE),
        in_specs=[
            pl.BlockSpec((1, MIXER_TILE, d_model), lambda b, j: (b, j, 0)),
            resident,
            pl.BlockSpec(memory_space=pl.ANY),
            resident,
            resident,
            resident,
            resident,
            resident,
            resident,
            pl.BlockSpec((1, width, mem_len), lambda b, j: (b, 0, 0)),
            pl.BlockSpec((1, mem_len, width), lambda b, j: (b, 0, 0)),
            resident,
        ],
        out_specs=pl.BlockSpec((1, MIXER_TILE, 3 * width), lambda b, j: (b, j, 0)),
        out_shape=jax.ShapeDtypeStruct((batch, seq, 3 * width), BF16),
        scratch_shapes=[pltpu.VMEM((POOL_HALO, width), F32)] + _stage_scratch(*w_in.shape),
        compiler_params=pltpu.CompilerParams(
            dimension_semantics=("arbitrary", "arbitrary"), vmem_limit_bytes=MIXER_VMEM_BYTES),
        name="mixers",
    )(x, g_pre, w_in, pool_w, pool_scale, ln_g, ln_b, sgu_w, sgu_bt, kt, v, g_branch)


def _out_project_kernel(y_ref, x_ref, w_out_hbm, g_post_ref, o_ref, w_out_ref, stage_ref, sem):
    @pl.when(pl.program_id(0) == 0)
    def _():
        _load_weight_bf16(w_out_hbm, w_out_ref, stage_ref, sem)

    out = _dot(y_ref[...], w_out_ref[...])
    o_ref[...] = x_ref[...] + _rms_scale(out) * g_post_ref[...]


def _out_project(y, x, w_out, g_post):
    tokens, d_model = x.shape
    mix_width = y.shape[-1]
    return pl.pallas_call(
        _out_project_kernel,
        grid=(tokens // OUT_TILE,),
        in_specs=[
            pl.BlockSpec((OUT_TILE, mix_width), lambda i: (i, 0)),
            pl.BlockSpec((OUT_TILE, d_model), lambda i: (i, 0)),
            pl.BlockSpec(memory_space=pl.ANY),
            pl.BlockSpec(memory_space=pltpu.VMEM),
        ],
        out_specs=pl.BlockSpec((OUT_TILE, d_model), lambda i: (i, 0)),
        out_shape=jax.ShapeDtypeStruct((tokens, d_model), F32),
        scratch_shapes=_stage_scratch(*w_out.shape),
        compiler_params=pltpu.CompilerParams(
            dimension_semantics=("arbitrary",), vmem_limit_bytes=OUT_VMEM_BYTES),
        name="out_project",
    )(y, x, w_out, g_post)


def kernel(x, mem, norm_pre, w_in, pool_w, pool_scale, sgu_ln_g, sgu_ln_b, sgu_w, sgu_b,
           mem_norm, w_kv, branch_norm, w_out, norm_post):
    batch, seq, d_model = x.shape
    depth = w_in.shape[0]
    assert seq % MIXER_TILE == 0 and MIXER_TILE % MIXER_ROWS == 0
    assert MIXER_ROWS % SGU_CHUNK == 0 and MIXER_ROWS >= POOL_HALO
    assert (batch * seq) % OUT_TILE == 0
    row = lambda a: a.reshape(1, -1)
    for l in range(depth):
        kt, v = _memory_kv(mem, mem_norm, w_kv[l])
        y = _mixers(x, row(norm_pre[l]), w_in[l], pool_w[l], row(pool_scale[l]),
                    row(sgu_ln_g[l]), row(sgu_ln_b[l]), sgu_w[l], jnp.transpose(sgu_b[l]),
                    kt, v, row(branch_norm[l]))
        x = _out_project(y.reshape(batch * seq, -1), x.reshape(batch * seq, d_model),
                         w_out[l], row(norm_post[l])).reshape(batch, seq, d_model)
    return x
```

```python
import math

import jax
import jax.numpy as jnp
from jax import lax
from jax.experimental import pallas as pl
from jax.experimental.pallas import tpu as pltpu

EPS = 1e-6
LOG2_E = 1.4426950408889634
POOL_WINDOWS = (2, 4, 8, 16)
POOL_HALO = 16
SGU_CHUNK = 128
SGU_HEADS = 8
XATTN_HEADS = 4

MIXER_TILE = 512
MIXER_ROWS = 512
OUT_TILE = 512
MIB = 1024 * 1024
MIXER_VMEM_BYTES = 62 * MIB
OUT_VMEM_BYTES = 62 * MIB
KV_VMEM_BYTES = 62 * MIB

STAGE_SLOTS = 8
STAGE_BYTES = 512 * 1024
BF16_SUBLANES = 16

F32 = jnp.float32
BF16 = jnp.bfloat16


def _rms_scale(v):
    return v * lax.rsqrt(jnp.mean(v * v, axis=-1, keepdims=True) + EPS)


def _silu(v):
    return v / (1.0 + jnp.exp2(v * -LOG2_E))


def _dot(a, b):
    return jnp.dot(a, b, preferred_element_type=F32)


def _stage_scratch(n_rows, n_cols):
    rows = max(BF16_SUBLANES, STAGE_BYTES // (4 * n_cols) // BF16_SUBLANES * BF16_SUBLANES)
    assert n_rows % rows == 0
    return [pltpu.VMEM((n_rows, n_cols), BF16),
            pltpu.VMEM((STAGE_SLOTS, rows, n_cols), F32),
            pltpu.SemaphoreType.DMA((STAGE_SLOTS,))]


def _load_weight_bf16(w_hbm, w_vmem, stage, sem):
    rows = stage.shape[1]
    n_chunks = w_hbm.shape[0] // rows

    def chunk_copy(i, slot):
        return pltpu.make_async_copy(w_hbm.at[pl.ds(i * rows, rows), :], stage.at[slot], sem.at[slot])

    for i in range(min(STAGE_SLOTS - 1, n_chunks)):
        chunk_copy(i, i).start()

    def body(i, carry):
        ahead = i + STAGE_SLOTS - 1

        @pl.when(ahead < n_chunks)
        def _():
            chunk_copy(ahead, ahead % STAGE_SLOTS).start()

        slot = i % STAGE_SLOTS
        chunk_copy(i, slot).wait()
        w_vmem[pl.ds(pl.multiple_of(i * rows, rows), rows), :] = stage[slot].astype(BF16)
        return carry

    lax.fori_loop(0, n_chunks, body, 0)


def _memory_kv_kernel(mem_ref, g_ref, wkv_hbm, kt_ref, v_ref, wkv_ref, stage_ref, sem):
    @pl.when(pl.program_id(0) == 0)
    def _():
        _load_weight_bf16(wkv_hbm, wkv_ref, stage_ref, sem)

    width = v_ref.shape[-1]
    mem_n = (_rms_scale(mem_ref[0]) * g_ref[...]).astype(BF16)
    kv = _dot(mem_n, wkv_ref[...])
    k = kv[:, :width] * (1.0 / math.sqrt(width // XATTN_HEADS))
    kt_ref[0] = k.T.astype(BF16)
    v_ref[0] = kv[:, width:].astype(BF16)


def _memory_kv(mem, mem_norm, w_kv):
    batch, mem_len, d_model = mem.shape
    width = w_kv.shape[1] // 2
    resident = pl.BlockSpec(memory_space=pltpu.VMEM)
    return pl.pallas_call(
        _memory_kv_kernel,
        grid=(batch,),
        in_specs=[
            pl.BlockSpec((1, mem_len, d_model), lambda b: (b, 0, 0)),
            resident,
            pl.BlockSpec(memory_space=pl.ANY),
        ],
        out_specs=[
            pl.BlockSpec((1, width, mem_len), lambda b: (b, 0, 0)),
            pl.BlockSpec((1, mem_len, width), lambda b: (b, 0, 0)),
        ],
        out_shape=[
            jax.ShapeDtypeStruct((batch, width, mem_len), BF16),
            jax.ShapeDtypeStruct((batch, mem_len, width), BF16),
        ],
        scratch_shapes=_stage_scratch(*w_kv.shape),
        compiler_params=pltpu.CompilerParams(
            dimension_semantics=("arbitrary",), vmem_limit_bytes=KV_VMEM_BYTES),
        name="memory_kv",
    )(mem, mem_norm.reshape(1, d_model), w_kv)


def _mix_rows(h, prev_rows, first_pos, w_in_ref, pool_w_ref, pool_scale_ref, ln_g_ref, ln_b_ref,
              sgu_w_ref, sgu_bt_ref, kt_ref, v_ref, g_br_ref, y_ref, row0):
    rows_n = h.shape[0]
    width = pool_scale_ref.shape[-1]
    out_rows = pl.ds(row0, rows_n)

    pa = _dot(h, w_in_ref[:, 0:2 * width])
    xa, ga = pa[:, :width], pa[:, width:]
    rows = jnp.concatenate([prev_rows, xa], axis=0)
    seen = (first_pos + 1 + lax.broadcasted_iota(jnp.int32, (rows_n, 1), 0)).astype(F32)
    group_dim = width // len(POOL_WINDOWS)
    ya_groups = []
    for g, window in enumerate(POOL_WINDOWS):
        s = rows[:, g * group_dim:(g + 1) * group_dim]
        shift = 1
        while shift < window:
            s = s + pltpu.roll(s, shift, axis=0)
            shift *= 2
        xg = xa[:, g * group_dim:(g + 1) * group_dim]
        d = s[POOL_HALO:, :] / jnp.minimum(seen, float(window)) - xg
        ya_groups.append(_dot(d.astype(BF16), pool_w_ref[g].astype(BF16)))
    ya = jnp.concatenate(ya_groups, axis=-1) * pool_scale_ref[...] * _silu(ga)
    y_ref[0, out_rows, 0:width] = (_rms_scale(ya) * g_br_ref[:, 0:width]).astype(BF16)

    pb = _dot(h, w_in_ref[:, 2 * width:5 * width])
    u, vb, gb = pb[:, :width], pb[:, width:2 * width], pb[:, 2 * width:]
    vc = vb - jnp.mean(vb, axis=-1, keepdims=True)
    vn = vc * lax.rsqrt(jnp.mean(vc * vc, axis=-1, keepdims=True) + EPS)
    vn = (vn * ln_g_ref[...] + ln_b_ref[...]).astype(BF16)
    causal = (lax.broadcasted_iota(jnp.int32, (SGU_CHUNK, SGU_CHUNK), 0)
              >= lax.broadcasted_iota(jnp.int32, (SGU_CHUNK, SGU_CHUNK), 1))
    head_dim = width // SGU_HEADS
    z_heads = []
    for hd in range(SGU_HEADS):
        w_hd = jnp.where(causal, sgu_w_ref[hd], 0.0).astype(BF16)
        bias = sgu_bt_ref[:, hd:hd + 1]
        z_chunks = []
        for c in range(rows_n // SGU_CHUNK):
            v_blk = vn[c * SGU_CHUNK:(c + 1) * SGU_CHUNK, hd * head_dim:(hd + 1) * head_dim]
            z_chunks.append(_dot(w_hd, v_blk) + bias)
        z_heads.append(jnp.concatenate(z_chunks, axis=0))
    yb = u * jnp.concatenate(z_heads, axis=-1) * _silu(gb)
    y_ref[0, out_rows, width:2 * width] = (
        _rms_scale(yb) * g_br_ref[:, width:2 * width]).astype(BF16)

    pc = _dot(h, w_in_ref[:, 5 * width:7 * width])
    q, gc = pc[:, :width], pc[:, width:]
    attn_dim = width // XATTN_HEADS
    o_heads = []
    for hd in range(XATTN_HEADS):
        cols = slice(hd * attn_dim, (hd + 1) * attn_dim)
        s = _dot(q[:, cols].astype(BF16), kt_ref[0, cols, :])
        e = jnp.exp(s - jnp.max(s, axis=-1, keepdims=True))
        p = e / jnp.sum(e, axis=-1, keepdims=True)
        o_heads.append(_dot(p.astype(BF16), v_ref[0, :, cols]))
    yc = jnp.concatenate(o_heads, axis=-1) * _silu(gc)
    y_ref[0, out_rows, 2 * width:3 * width] = (
        _rms_scale(yc) * g_br_ref[:, 2 * width:3 * width]).astype(BF16)
    return xa[rows_n - POOL_HALO:, :]


def _mixers_kernel(x_ref, g_pre_ref, w_in_hbm, pool_w_ref, pool_scale_ref, ln_g_ref, ln_b_ref,
                   sgu_w_ref, sgu_bt_ref, kt_ref, v_ref, g_br_ref, y_ref,
                   halo_ref, w_in_ref, stage_ref, sem):
    tile = x_ref.shape[1]
    seq_tile = pl.program_id(1)

    @pl.when((pl.program_id(0) == 0) & (seq_tile == 0))
    def _():
        _load_weight_bf16(w_in_hbm, w_in_ref, stage_ref, sem)

    @pl.when(seq_tile == 0)
    def _():
        halo_ref[...] = jnp.zeros_like(halo_ref)

    prev_rows = halo_ref[...]
    for row0 in range(0, tile, MIXER_ROWS):
        h = (_rms_scale(x_ref[0, row0:row0 + MIXER_ROWS, :]) * g_pre_ref[...]).astype(BF16)
        prev_rows = _mix_rows(h, prev_rows, seq_tile * tile + row0, w_in_ref, pool_w_ref,
                              pool_scale_ref, ln_g_ref, ln_b_ref, sgu_w_ref, sgu_bt_ref, kt_ref,
                              v_ref, g_br_ref, y_ref, row0)
    halo_ref[...] = prev_rows


def _mixers(x, g_pre, w_in, pool_w, pool_scale, ln_g, ln_b, sgu_w, sgu_bt, kt, v, g_branch):
    batch, seq, d_model = x.shape
    width = pool_scale.shape[-1]
    mem_len = v.shape[1]
    resident = pl.BlockSpec(memory_space=pltpu.VMEM)
    return pl.pallas_call(
        _mixers_kernel,
        grid=(batch, seq // MIXER_TILE),
        in_specs=[
            pl.BlockSpec((1, MIXER_TILE, d_model), lambda b, j: (b, j, 0)),
            resident,
            pl.BlockSpec(memory_space=pl.ANY),
            resident,
            resident,
            resident,
            resident,
            resident,
            resident,
            pl.BlockSpec((1, width, mem_len), lambda b, j: (b, 0, 0)),
            pl.BlockSpec((1, mem_len, width), lambda b, j: (b, 0, 0)),
            resident,
        ],
        out_specs=pl.BlockSpec((1, MIXER_TILE, 3 * width), lambda b, j: (b, j, 0)),
        out_shape=jax.ShapeDtypeStruct((batch, seq, 3 * width), BF16),
        scratch_shapes=[pltpu.VMEM((POOL_HALO, width), F32)] + _stage_scratch(*w_in.shape),
        compiler_params=pltpu.CompilerParams(
            dimension_semantics=("arbitrary", "arbitrary"), vmem_limit_bytes=MIXER_VMEM_BYTES),
        name="mixers",
    )(x, g_pre, w_in, pool_w, pool_scale, ln_g, ln_b, sgu_w, sgu_bt, kt, v, g_branch)


def _out_project_kernel(y_ref, x_ref, w_out_hbm, g_post_ref, o_ref, w_out_ref, stage_ref, sem):
    @pl.when(pl.program_id(0) == 0)
    def _():
        _load_weight_bf16(w_out_hbm, w_out_ref, stage_ref, sem)

    out = _dot(y_ref[...], w_out_ref[...])
    o_ref[...] = x_ref[...] + _rms_scale(out) * g_post_ref[...]


def _out_project(y, x, w_out, g_post):
    tokens, d_model = x.shape
    mix_width = y.shape[-1]
    return pl.pallas_call(
        _out_project_kernel,
        grid=(tokens // OUT_TILE,),
        in_specs=[
            pl.BlockSpec((OUT_TILE, mix_width), lambda i: (i, 0)),
            pl.BlockSpec((OUT_TILE, d_model), lambda i: (i, 0)),
            pl.BlockSpec(memory_space=pl.ANY),
            pl.BlockSpec(memory_space=pltpu.VMEM),
        ],
        out_specs=pl.BlockSpec((OUT_TILE, d_model), lambda i: (i, 0)),
        out_shape=jax.ShapeDtypeStruct((tokens, d_model), F32),
        scratch_shapes=_stage_scratch(*w_out.shape),
        compiler_params=pltpu.CompilerParams(
            dimension_semantics=("arbitrary",), vmem_limit_bytes=OUT_VMEM_BYTES),
        name="out_project",
    )(y, x, w_out, g_post)


def kernel(x, mem, norm_pre, w_in, pool_w, pool_scale, sgu_ln_g, sgu_ln_b, sgu_w, sgu_b,
           mem_norm, w_kv, branch_norm, w_out, norm_post):
    batch, seq, d_model = x.shape
    depth = w_in.shape[0]
    assert seq % MIXER_TILE == 0 and MIXER_TILE % MIXER_ROWS == 0
    assert MIXER_ROWS % SGU_CHUNK == 0 and MIXER_ROWS >= POOL_HALO
    assert (batch * seq) % OUT_TILE == 0
    row = lambda a: a.reshape(1, -1)
    for l in range(depth):
        kt, v = _memory_kv(mem, mem_norm, w_kv[l])
        y = _mixers(x, row(norm_pre[l]), w_in[l], pool_w[l], row(pool_scale[l]),
                    row(sgu_ln_g[l]), row(sgu_ln_b[l]), sgu_w[l], jnp.transpose(sgu_b[l]),
                    kt, v, row(branch_norm[l]))
        x = _out_project(y.reshape(batch * seq, -1), x.reshape(batch * seq, d_model),
                         w_out[l], row(norm_post[l])).reshape(batch, seq, d_model)
    return x
```

```python
import math

import jax
import jax.numpy as jnp
from jax import lax
from jax.experimental import pallas as pl
from jax.experimental.pallas import tpu as pltpu

EPS = 1e-6
LOG2_E = 1.4426950408889634
POOL_WINDOWS = (2, 4, 8, 16)
POOL_HALO = 16
SGU_CHUNK = 128
SGU_HEADS = 8
XATTN_HEADS = 4

MIXER_TILE = 512
MIXER_ROWS = 256
OUT_TILE = 512
MIB = 1024 * 1024
MIXER_VMEM_BYTES = 62 * MIB
OUT_VMEM_BYTES = 62 * MIB
KV_VMEM_BYTES = 62 * MIB

STAGE_SLOTS = 12
STAGE_BYTES = 512 * 1024
BF16_SUBLANES = 16

F32 = jnp.float32
BF16 = jnp.bfloat16


def _rms_scale(v):
    return v * lax.rsqrt(jnp.mean(v * v, axis=-1, keepdims=True) + EPS)


def _silu(v):
    return v / (1.0 + jnp.exp2(v * -LOG2_E))


def _dot(a, b):
    return jnp.dot(a, b, preferred_element_type=F32)


def _stage_scratch(n_rows, n_cols):
    rows = max(BF16_SUBLANES, STAGE_BYTES // (4 * n_cols) // BF16_SUBLANES * BF16_SUBLANES)
    assert n_rows % rows == 0
    return [pltpu.VMEM((n_rows, n_cols), BF16),
            pltpu.VMEM((STAGE_SLOTS, rows, n_cols), F32),
            pltpu.SemaphoreType.DMA((STAGE_SLOTS,))]


def _load_weight_bf16(w_hbm, w_vmem, stage, sem):
    rows = stage.shape[1]
    n_chunks = w_hbm.shape[0] // rows

    def chunk_copy(i, slot):
        return pltpu.make_async_copy(w_hbm.at[pl.ds(i * rows, rows), :], stage.at[slot], sem.at[slot])

    for i in range(min(STAGE_SLOTS - 1, n_chunks)):
        chunk_copy(i, i).start()

    def body(i, carry):
        ahead = i + STAGE_SLOTS - 1

        @pl.when(ahead < n_chunks)
        def _():
            chunk_copy(ahead, ahead % STAGE_SLOTS).start()

        slot = i % STAGE_SLOTS
        chunk_copy(i, slot).wait()
        w_vmem[pl.ds(pl.multiple_of(i * rows, rows), rows), :] = stage[slot].astype(BF16)
        return carry

    lax.fori_loop(0, n_chunks, body, 0)


def _memory_kv_kernel(mem_ref, g_ref, wkv_hbm, kt_ref, v_ref, wkv_ref, stage_ref, sem):
    @pl.when(pl.program_id(0) == 0)
    def _():
        _load_weight_bf16(wkv_hbm, wkv_ref, stage_ref, sem)

    width = v_ref.shape[-1]
    mem_n = (_rms_scale(mem_ref[0]) * g_ref[...]).astype(BF16)
    kv = _dot(mem_n, wkv_ref[...])
    k = kv[:, :width] * (1.0 / math.sqrt(width // XATTN_HEADS))
    kt_ref[0] = k.T.astype(BF16)
    v_ref[0] = kv[:, width:].astype(BF16)


def _memory_kv(mem, mem_norm, w_kv):
    batch, mem_len, d_model = mem.shape
    width = w_kv.shape[1] // 2
    resident = pl.BlockSpec(memory_space=pltpu.VMEM)
    return pl.pallas_call(
        _memory_kv_kernel,
        grid=(batch,),
        in_specs=[
            pl.BlockSpec((1, mem_len, d_model), lambda b: (b, 0, 0)),
            resident,
            pl.BlockSpec(memory_space=pl.ANY),
        ],
        out_specs=[
            pl.BlockSpec((1, width, mem_len), lambda b: (b, 0, 0)),
            pl.BlockSpec((1, mem_len, width), lambda b: (b, 0, 0)),
        ],
        out_shape=[
            jax.ShapeDtypeStruct((batch, width, mem_len), BF16),
            jax.ShapeDtypeStruct((batch, mem_len, width), BF16),
        ],
        scratch_shapes=_stage_scratch(*w_kv.shape),
        compiler_params=pltpu.CompilerParams(
            dimension_semantics=("arbitrary",), vmem_limit_bytes=KV_VMEM_BYTES),
        name="memory_kv",
    )(mem, mem_norm.reshape(1, d_model), w_kv)


def _mix_rows(h, prev_rows, first_pos, w_in_ref, pool_w_ref, pool_scale_ref, ln_g_ref, ln_b_ref,
              sgu_w_ref, sgu_b_ref, kt_ref, v_ref, g_br_ref, y_ref, row0):
    rows_n = h.shape[0]
    width = pool_scale_ref.shape[-1]
    out_rows = pl.ds(row0, rows_n)

    pa = _dot(h, w_in_ref[:, 0:2 * width])
    xa, ga = pa[:, :width], pa[:, width:]
    rows = jnp.concatenate([prev_rows, xa], axis=0)
    seen = (first_pos + 1 + lax.broadcasted_iota(jnp.int32, (rows_n, 1), 0)).astype(F32)
    group_dim = width // len(POOL_WINDOWS)
    ya_groups = []
    for g, window in enumerate(POOL_WINDOWS):
        s = rows[:, g * group_dim:(g + 1) * group_dim]
        shift = 1
        while shift < window:
            s = s + pltpu.roll(s, shift, axis=0)
            shift *= 2
        xg = xa[:, g * group_dim:(g + 1) * group_dim]
        d = s[POOL_HALO:, :] / jnp.minimum(seen, float(window)) - xg
        ya_groups.append(_dot(d.astype(BF16), pool_w_ref[g].astype(BF16)))
    ya = jnp.concatenate(ya_groups, axis=-1) * pool_scale_ref[...] * _silu(ga)
    y_ref[0, out_rows, 0:width] = (_rms_scale(ya) * g_br_ref[:, 0:width]).astype(BF16)

    pb = _dot(h, w_in_ref[:, 2 * width:5 * width])
    u, vb, gb = pb[:, :width], pb[:, width:2 * width], pb[:, 2 * width:]
    vc = vb - jnp.mean(vb, axis=-1, keepdims=True)
    vn = vc * lax.rsqrt(jnp.mean(vc * vc, axis=-1, keepdims=True) + EPS)
    vn = (vn * ln_g_ref[...] + ln_b_ref[...]).astype(BF16)
    causal = (lax.broadcasted_iota(jnp.int32, (SGU_CHUNK, SGU_CHUNK), 0)
              >= lax.broadcasted_iota(jnp.int32, (SGU_CHUNK, SGU_CHUNK), 1))
    head_dim = width // SGU_HEADS
    sgu_bt = sgu_b_ref[...].T
    z_heads = []
    for hd in range(SGU_HEADS):
        w_hd = jnp.where(causal, sgu_w_ref[hd], 0.0).astype(BF16)
        bias = sgu_bt[:, hd:hd + 1]
        z_chunks = []
        for c in range(rows_n // SGU_CHUNK):
            v_blk = vn[c * SGU_CHUNK:(c + 1) * SGU_CHUNK, hd * head_dim:(hd + 1) * head_dim]
            z_chunks.append(_dot(w_hd, v_blk) + bias)
        z_heads.append(jnp.concatenate(z_chunks, axis=0))
    yb = u * jnp.concatenate(z_heads, axis=-1) * _silu(gb)
    y_ref[0, out_rows, width:2 * width] = (
        _rms_scale(yb) * g_br_ref[:, width:2 * width]).astype(BF16)

    pc = _dot(h, w_in_ref[:, 5 * width:7 * width])
    q, gc = pc[:, :width], pc[:, width:]
    attn_dim = width // XATTN_HEADS
    o_heads = []
    for hd in range(XATTN_HEADS):
        cols = slice(hd * attn_dim, (hd + 1) * attn_dim)
        s = _dot(q[:, cols].astype(BF16), kt_ref[0, cols, :])
        e = jnp.exp(s - jnp.max(s, axis=-1, keepdims=True))
        p = e / jnp.sum(e, axis=-1, keepdims=True)
        o_heads.append(_dot(p.astype(BF16), v_ref[0, :, cols]))
    yc = jnp.concatenate(o_heads, axis=-1) * _silu(gc)
    y_ref[0, out_rows, 2 * width:3 * width] = (
        _rms_scale(yc) * g_br_ref[:, 2 * width:3 * width]).astype(BF16)
    return xa[rows_n - POOL_HALO:, :]


def _mixers_kernel(x_ref, g_pre_ref, w_in_hbm, pool_w_ref, pool_scale_ref, ln_g_ref, ln_b_ref,
                   sgu_w_ref, sgu_b_ref, kt_ref, v_ref, g_br_ref, y_ref,
                   halo_ref, w_in_ref, stage_ref, sem):
    tile = x_ref.shape[1]
    seq_tile = pl.program_id(1)

    @pl.when((pl.program_id(0) == 0) & (seq_tile == 0))
    def _():
        _load_weight_bf16(w_in_hbm, w_in_ref, stage_ref, sem)

    @pl.when(seq_tile == 0)
    def _():
        halo_ref[...] = jnp.zeros_like(halo_ref)

    prev_rows = halo_ref[...]
    for row0 in range(0, tile, MIXER_ROWS):
        h = (_rms_scale(x_ref[0, row0:row0 + MIXER_ROWS, :]) * g_pre_ref[...]).astype(BF16)
        prev_rows = _mix_rows(h, prev_rows, seq_tile * tile + row0, w_in_ref, pool_w_ref,
                              pool_scale_ref, ln_g_ref, ln_b_ref, sgu_w_ref, sgu_b_ref, kt_ref,
                              v_ref, g_br_ref, y_ref, row0)
    halo_ref[...] = prev_rows


def _mixers(x, g_pre, w_in, pool_w, pool_scale, ln_g, ln_b, sgu_w, sgu_b, kt, v, g_branch):
    batch, seq, d_model = x.shape
    width = pool_scale.shape[-1]
    mem_len = v.shape[1]
    resident = pl.BlockSpec(memory_space=pltpu.VMEM)
    return pl.pallas_call(
        _mixers_kernel,
        grid=(batch, seq // MIXER_TILE),
        in_specs=[
            pl.BlockSpec((1, MIXER_TILE, d_model), lambda b, j: (b, j, 0)),
            resident,
            pl.BlockSpec(memory_space=pl.ANY),
            resident,
            resident,
            resident,
            resident,
            resident,
            resident,
            pl.BlockSpec((1, width, mem_len), lambda b, j: (b, 0, 0)),
            pl.BlockSpec((1, mem_len, width), lambda b, j: (b, 0, 0)),
            resident,
        ],
        out_specs=pl.BlockSpec((1, MIXER_TILE, 3 * width), lambda b, j: (b, j, 0)),
        out_shape=jax.ShapeDtypeStruct((batch, seq, 3 * width), BF16),
        scratch_shapes=[pltpu.VMEM((POOL_HALO, width), F32)] + _stage_scratch(*w_in.shape),
        compiler_params=pltpu.CompilerParams(
            dimension_semantics=("arbitrary", "arbitrary"), vmem_limit_bytes=MIXER_VMEM_BYTES),
        name="mixers",
    )(x, g_pre, w_in, pool_w, pool_scale, ln_g, ln_b, sgu_w, sgu_b, kt, v, g_branch)


def _out_project_kernel(y_ref, x_ref, w_out_hbm, g_post_ref, o_ref, w_out_ref, stage_ref, sem):
    @pl.when(pl.program_id(0) == 0)
    def _():
        _load_weight_bf16(w_out_hbm, w_out_ref, stage_ref, sem)

    out = _dot(y_ref[...], w_out_ref[...])
    o_ref[...] = x_ref[...] + _rms_scale(out) * g_post_ref[...]


def _out_project(y, x, w_out, g_post):
    tokens, d_model = x.shape
    mix_width = y.shape[-1]
    return pl.pallas_call(
        _out_project_kernel,
        grid=(tokens // OUT_TILE,),
        in_specs=[
            pl.BlockSpec((OUT_TILE, mix_width), lambda i: (i, 0)),
            pl.BlockSpec((OUT_TILE, d_model), lambda i: (i, 0)),
            pl.BlockSpec(memory_space=pl.ANY),
            pl.BlockSpec(memory_space=pltpu.VMEM),
        ],
        out_specs=pl.BlockSpec((OUT_TILE, d_model), lambda i: (i, 0)),
        out_shape=jax.ShapeDtypeStruct((tokens, d_model), F32),
        scratch_shapes=_stage_scratch(*w_out.shape),
        compiler_params=pltpu.CompilerParams(
            dimension_semantics=("arbitrary",), vmem_limit_bytes=OUT_VMEM_BYTES),
        name="out_project",
    )(y, x, w_out, g_post)


def kernel(x, mem, norm_pre, w_in, pool_w, pool_scale, sgu_ln_g, sgu_ln_b, sgu_w, sgu_b,
           mem_norm, w_kv, branch_norm, w_out, norm_post):
    batch, seq, d_model = x.shape
    depth = w_in.shape[0]
    assert seq % MIXER_TILE == 0 and MIXER_TILE % MIXER_ROWS == 0
    assert MIXER_ROWS % SGU_CHUNK == 0 and MIXER_ROWS >= POOL_HALO
    assert (batch * seq) % OUT_TILE == 0
    row = lambda a: a.reshape(1, -1)
    for l in range(depth):
        kt, v = _memory_kv(mem, mem_norm, w_kv[l])
        y = _mixers(x, row(norm_pre[l]), w_in[l], pool_w[l], row(pool_scale[l]),
                    row(sgu_ln_g[l]), row(sgu_ln_b[l]), sgu_w[l], sgu_b[l],
                    kt, v, row(branch_norm[l]))
        x = _out_project(y.reshape(batch * seq, -1), x.reshape(batch * seq, d_model),
                         w_out[l], row(norm_post[l])).reshape(batch, seq, d_model)
    return x
```

```python
import math

import jax
import jax.numpy as jnp
from jax import lax
from jax.experimental import pallas as pl
from jax.experimental.pallas import tpu as pltpu

EPS = 1e-6
LOG2_E = 1.4426950408889634
POOL_WINDOWS = (2, 4, 8, 16)
POOL_HALO = 16
SGU_CHUNK = 128
SGU_HEADS = 8
XATTN_HEADS = 4

MIXER_TILE = 512
MIXER_ROWS = 256
OUT_TILE = 512
MIB = 1024 * 1024
MIXER_VMEM_BYTES = 62 * MIB
OUT_VMEM_BYTES = 62 * MIB
KV_VMEM_BYTES = 62 * MIB

STAGE_SLOTS = 16
MIXER_STAGE_SLOTS = 14
STAGE_BYTES = 512 * 1024
BF16_SUBLANES = 16

F32 = jnp.float32
BF16 = jnp.bfloat16


def _rms_scale(v):
    return v * lax.rsqrt(jnp.mean(v * v, axis=-1, keepdims=True) + EPS)


def _silu(v):
    return v / (1.0 + jnp.exp2(v * -LOG2_E))


def _dot(a, b):
    return jnp.dot(a, b, preferred_element_type=F32)


def _stage_scratch(n_rows, n_cols, slots=STAGE_SLOTS):
    rows = max(BF16_SUBLANES, STAGE_BYTES // (4 * n_cols) // BF16_SUBLANES * BF16_SUBLANES)
    assert n_rows % rows == 0
    return [pltpu.VMEM((n_rows, n_cols), BF16),
            pltpu.VMEM((slots, rows, n_cols), F32),
            pltpu.SemaphoreType.DMA((slots,))]


def _load_weight_bf16(w_hbm, w_vmem, stage, sem):
    slots, rows = stage.shape[0], stage.shape[1]
    n_chunks = w_hbm.shape[0] // rows

    def chunk_copy(i, slot):
        return pltpu.make_async_copy(w_hbm.at[pl.ds(i * rows, rows), :], stage.at[slot], sem.at[slot])

    for i in range(min(slots - 1, n_chunks)):
        chunk_copy(i, i).start()

    def body(i, carry):
        ahead = i + slots - 1

        @pl.when(ahead < n_chunks)
        def _():
            chunk_copy(ahead, ahead % slots).start()

        slot = i % slots
        chunk_copy(i, slot).wait()
        w_vmem[pl.ds(pl.multiple_of(i * rows, rows), rows), :] = stage[slot].astype(BF16)
        return carry

    lax.fori_loop(0, n_chunks, body, 0)


def _memory_kv_kernel(mem_ref, g_ref, wkv_hbm, kt_ref, v_ref, wkv_ref, stage_ref, sem):
    @pl.when(pl.program_id(0) == 0)
    def _():
        _load_weight_bf16(wkv_hbm, wkv_ref, stage_ref, sem)

    width = v_ref.shape[-1]
    mem_n = (_rms_scale(mem_ref[0]) * g_ref[...]).astype(BF16)
    kv = _dot(mem_n, wkv_ref[...])
    k = kv[:, :width] * (1.0 / math.sqrt(width // XATTN_HEADS))
    kt_ref[0] = k.T.astype(BF16)
    v_ref[0] = kv[:, width:].astype(BF16)


def _memory_kv(mem, mem_norm, w_kv):
    batch, mem_len, d_model = mem.shape
    width = w_kv.shape[1] // 2
    resident = pl.BlockSpec(memory_space=pltpu.VMEM)
    return pl.pallas_call(
        _memory_kv_kernel,
        grid=(batch,),
        in_specs=[
            pl.BlockSpec((1, mem_len, d_model), lambda b: (b, 0, 0)),
            resident,
            pl.BlockSpec(memory_space=pl.ANY),
        ],
        out_specs=[
            pl.BlockSpec((1, width, mem_len), lambda b: (b, 0, 0)),
            pl.BlockSpec((1, mem_len, width), lambda b: (b, 0, 0)),
        ],
        out_shape=[
            jax.ShapeDtypeStruct((batch, width, mem_len), BF16),
            jax.ShapeDtypeStruct((batch, mem_len, width), BF16),
        ],
        scratch_shapes=_stage_scratch(*w_kv.shape),
        compiler_params=pltpu.CompilerParams(
            dimension_semantics=("arbitrary",), vmem_limit_bytes=KV_VMEM_BYTES),
        name="memory_kv",
    )(mem, mem_norm.reshape(1, d_model), w_kv)


def _mix_rows(h, prev_rows, first_pos, w_in_ref, pool_w_ref, pool_scale_ref, ln_g_ref, ln_b_ref,
              sgu_w_ref, sgu_b_ref, kt_ref, v_ref, g_br_ref, y_ref, row0):
    rows_n = h.shape[0]
    width = pool_scale_ref.shape[-1]
    out_rows = pl.ds(row0, rows_n)

    pa = _dot(h, w_in_ref[:, 0:2 * width])
    xa, ga = pa[:, :width], pa[:, width:]
    rows = jnp.concatenate([prev_rows, xa], axis=0)
    seen = (first_pos + 1 + lax.broadcasted_iota(jnp.int32, (rows_n, 1), 0)).astype(F32)
    group_dim = width // len(POOL_WINDOWS)
    ya_groups = []
    for g, window in enumerate(POOL_WINDOWS):
        s = rows[:, g * group_dim:(g + 1) * group_dim]
        shift = 1
        while shift < window:
            s = s + pltpu.roll(s, shift, axis=0)
            shift *= 2
        xg = xa[:, g * group_dim:(g + 1) * group_dim]
        d = s[POOL_HALO:, :] / jnp.minimum(seen, float(window)) - xg
        ya_groups.append(_dot(d.astype(BF16), pool_w_ref[g].astype(BF16)))
    ya = jnp.concatenate(ya_groups, axis=-1) * pool_scale_ref[...] * _silu(ga)
    y_ref[0, out_rows, 0:width] = (_rms_scale(ya) * g_br_ref[:, 0:width]).astype(BF16)

    pb = _dot(h, w_in_ref[:, 2 * width:5 * width])
    u, vb, gb = pb[:, :width], pb[:, width:2 * width], pb[:, 2 * width:]
    vc = vb - jnp.mean(vb, axis=-1, keepdims=True)
    vn = vc * lax.rsqrt(jnp.mean(vc * vc, axis=-1, keepdims=True) + EPS)
    vn = (vn * ln_g_ref[...] + ln_b_ref[...]).astype(BF16)
    causal = (lax.broadcasted_iota(jnp.int32, (SGU_CHUNK, SGU_CHUNK), 0)
              >= lax.broadcasted_iota(jnp.int32, (SGU_CHUNK, SGU_CHUNK), 1))
    head_dim = width // SGU_HEADS
    sgu_bt = sgu_b_ref[...].T
    z_heads = []
    for hd in range(SGU_HEADS):
        w_hd = jnp.where(causal, sgu_w_ref[hd], 0.0).astype(BF16)
        bias = sgu_bt[:, hd:hd + 1]
        z_chunks = []
        for c in range(rows_n // SGU_CHUNK):
            v_blk = vn[c * SGU_CHUNK:(c + 1) * SGU_CHUNK, hd * head_dim:(hd + 1) * head_dim]
            z_chunks.append(_dot(w_hd, v_blk) + bias)
        z_heads.append(jnp.concatenate(z_chunks, axis=0))
    yb = u * jnp.concatenate(z_heads, axis=-1) * _silu(gb)
    y_ref[0, out_rows, width:2 * width] = (
        _rms_scale(yb) * g_br_ref[:, width:2 * width]).astype(BF16)

    pc = _dot(h, w_in_ref[:, 5 * width:7 * width])
    q, gc = pc[:, :width], pc[:, width:]
    attn_dim = width // XATTN_HEADS
    o_heads = []
    for hd in range(XATTN_HEADS):
        cols = slice(hd * attn_dim, (hd + 1) * attn_dim)
        s = _dot(q[:, cols].astype(BF16), kt_ref[0, cols, :])
        e = jnp.exp(s - jnp.max(s, axis=-1, keepdims=True))
        p = e / jnp.sum(e, axis=-1, keepdims=True)
        o_heads.append(_dot(p.astype(BF16), v_ref[0, :, cols]))
    yc = jnp.concatenate(o_heads, axis=-1) * _silu(gc)
    y_ref[0, out_rows, 2 * width:3 * width] = (
        _rms_scale(yc) * g_br_ref[:, 2 * width:3 * width]).astype(BF16)
    return xa[rows_n - POOL_HALO:, :]


def _mixers_kernel(x_ref, g_pre_ref, w_in_hbm, pool_w_ref, pool_scale_ref, ln_g_ref, ln_b_ref,
                   sgu_w_ref, sgu_b_ref, kt_ref, v_ref, g_br_ref, y_ref,
                   halo_ref, w_in_ref, stage_ref, sem):
    tile = x_ref.shape[1]
    seq_tile = pl.program_id(1)

    @pl.when((pl.program_id(0) == 0) & (seq_tile == 0))
    def _():
        _load_weight_bf16(w_in_hbm, w_in_ref, stage_ref, sem)

    @pl.when(seq_tile == 0)
    def _():
        halo_ref[...] = jnp.zeros_like(halo_ref)

    prev_rows = halo_ref[...]
    for row0 in range(0, tile, MIXER_ROWS):
        h = (_rms_scale(x_ref[0, row0:row0 + MIXER_ROWS, :]) * g_pre_ref[...]).astype(BF16)
        prev_rows = _mix_rows(h, prev_rows, seq_tile * tile + row0, w_in_ref, pool_w_ref,
                              pool_scale_ref, ln_g_ref, ln_b_ref, sgu_w_ref, sgu_b_ref, kt_ref,
                              v_ref, g_br_ref, y_ref, row0)
    halo_ref[...] = prev_rows


def _mixers(x, g_pre, w_in, pool_w, pool_scale, ln_g, ln_b, sgu_w, sgu_b, kt, v, g_branch):
    batch, seq, d_model = x.shape
    width = pool_scale.shape[-1]
    mem_len = v.shape[1]
    resident = pl.BlockSpec(memory_space=pltpu.VMEM)
    return pl.pallas_call(
        _mixers_kernel,
        grid=(batch, seq // MIXER_TILE),
        in_specs=[
            pl.BlockSpec((1, MIXER_TILE, d_model), lambda b, j: (b, j, 0)),
            resident,
            pl.BlockSpec(memory_space=pl.ANY),
            resident,
            resident,
            resident,
            resident,
            resident,
            resident,
            pl.BlockSpec((1, width, mem_len), lambda b, j: (b, 0, 0), pipeline_mode=pl.Buffered(1)),
            pl.BlockSpec((1, mem_len, width), lambda b, j: (b, 0, 0), pipeline_mode=pl.Buffered(1)),
            resident,
        ],
        out_specs=pl.BlockSpec((1, MIXER_TILE, 3 * width), lambda b, j: (b, j, 0)),
        out_shape=jax.ShapeDtypeStruct((batch, seq, 3 * width), BF16),
        scratch_shapes=[pltpu.VMEM((POOL_HALO, width), F32)] + _stage_scratch(*w_in.shape, MIXER_STAGE_SLOTS),
        compiler_params=pltpu.CompilerParams(
            dimension_semantics=("arbitrary", "arbitrary"), vmem_limit_bytes=MIXER_VMEM_BYTES),
        name="mixers",
    )(x, g_pre, w_in, pool_w, pool_scale, ln_g, ln_b, sgu_w, sgu_b, kt, v, g_branch)


def _out_project_kernel(y_ref, x_ref, w_out_hbm, g_post_ref, o_ref, w_out_ref, stage_ref, sem):
    @pl.when(pl.program_id(0) == 0)
    def _():
        _load_weight_bf16(w_out_hbm, w_out_ref, stage_ref, sem)

    out = _dot(y_ref[...], w_out_ref[...])
    o_ref[...] = x_ref[...] + _rms_scale(out) * g_post_ref[...]


def _out_project(y, x, w_out, g_post):
    tokens, d_model = x.shape
    mix_width = y.shape[-1]
    return pl.pallas_call(
        _out_project_kernel,
        grid=(tokens // OUT_TILE,),
        in_specs=[
            pl.BlockSpec((OUT_TILE, mix_width), lambda i: (i, 0)),
            pl.BlockSpec((OUT_TILE, d_model), lambda i: (i, 0)),
            pl.BlockSpec(memory_space=pl.ANY),
            pl.BlockSpec(memory_space=pltpu.VMEM),
        ],
        out_specs=pl.BlockSpec((OUT_TILE, d_model), lambda i: (i, 0)),
        out_shape=jax.ShapeDtypeStruct((tokens, d_model), F32),
        scratch_shapes=_stage_scratch(*w_out.shape),
        compiler_params=pltpu.CompilerParams(
            dimension_semantics=("arbitrary",), vmem_limit_bytes=OUT_VMEM_BYTES),
        name="out_project",
    )(y, x, w_out, g_post)


def kernel(x, mem, norm_pre, w_in, pool_w, pool_scale, sgu_ln_g, sgu_ln_b, sgu_w, sgu_b,
           mem_norm, w_kv, branch_norm, w_out, norm_post):
    batch, seq, d_model = x.shape
    depth = w_in.shape[0]
    assert seq % MIXER_TILE == 0 and MIXER_TILE % MIXER_ROWS == 0
    assert MIXER_ROWS % SGU_CHUNK == 0 and MIXER_ROWS >= POOL_HALO
    assert (batch * seq) % OUT_TILE == 0
    row = lambda a: a.reshape(1, -1)
    for l in range(depth):
        kt, v = _memory_kv(mem, mem_norm, w_kv[l])
        y = _mixers(x, row(norm_pre[l]), w_in[l], pool_w[l], row(pool_scale[l]),
                    row(sgu_ln_g[l]), row(sgu_ln_b[l]), sgu_w[l], sgu_b[l],
                    kt, v, row(branch_norm[l]))
        x = _out_project(y.reshape(batch * seq, -1), x.reshape(batch * seq, d_model),
                         w_out[l], row(norm_post[l])).reshape(batch, seq, d_model)
    return x
```

```python
import math

import jax
import jax.numpy as jnp
from jax import lax
from jax.experimental import pallas as pl
from jax.experimental.pallas import tpu as pltpu

EPS = 1e-6
LOG2_E = 1.4426950408889634
POOL_WINDOWS = (2, 4, 8, 16)
POOL_HALO = 16
SGU_CHUNK = 128
SGU_HEADS = 8
XATTN_HEADS = 4

MIXER_TILE = 512
MIXER_ROWS = 256
OUT_TILE = 512
MIB = 1024 * 1024
MIXER_VMEM_BYTES = 62 * MIB
OUT_VMEM_BYTES = 62 * MIB
KV_VMEM_BYTES = 62 * MIB

STAGE_SLOTS = 12
STAGE_BYTES = 512 * 1024
BF16_SUBLANES = 16

F32 = jnp.float32
BF16 = jnp.bfloat16


def _rms_scale(v):
    return v * lax.rsqrt(jnp.mean(v * v, axis=-1, keepdims=True) + EPS)


def _silu(v):
    return v / (1.0 + jnp.exp2(v * -LOG2_E))


def _dot(a, b):
    return jnp.dot(a, b, preferred_element_type=F32)


def _stage_scratch(n_rows, n_cols):
    rows = max(BF16_SUBLANES, STAGE_BYTES // (4 * n_cols) // BF16_SUBLANES * BF16_SUBLANES)
    assert n_rows % rows == 0
    return [pltpu.VMEM((n_rows, n_cols), BF16),
            pltpu.VMEM((STAGE_SLOTS, rows, n_cols), F32),
            pltpu.SemaphoreType.DMA((STAGE_SLOTS,))]


def _load_weight_bf16(w_hbm, w_vmem, stage, sem):
    slots, rows = stage.shape[0], stage.shape[1]
    n_chunks = w_hbm.shape[0] // rows

    def chunk_copy(i, slot):
        return pltpu.make_async_copy(w_hbm.at[pl.ds(i * rows, rows), :], stage.at[slot], sem.at[slot])

    for i in range(min(slots - 1, n_chunks)):
        chunk_copy(i, i).start()

    def body(i, carry):
        ahead = i + slots - 1

        @pl.when(ahead < n_chunks)
        def _():
            chunk_copy(ahead, ahead % slots).start()

        slot = i % slots
        chunk_copy(i, slot).wait()
        w_vmem[pl.ds(pl.multiple_of(i * rows, rows), rows), :] = stage[slot].astype(BF16)
        return carry

    lax.fori_loop(0, n_chunks, body, 0)


def _memory_kv_kernel(mem_ref, g_ref, wkv_hbm, kt_ref, v_ref, wkv_ref, stage_ref, sem):
    @pl.when(pl.program_id(0) == 0)
    def _():
        _load_weight_bf16(wkv_hbm, wkv_ref, stage_ref, sem)

    width = v_ref.shape[-1]
    mem_n = (_rms_scale(mem_ref[0]) * g_ref[...]).astype(BF16)
    kv = _dot(mem_n, wkv_ref[...])
    k = kv[:, :width] * (1.0 / math.sqrt(width // XATTN_HEADS))
    kt_ref[0] = k.T.astype(BF16)
    v_ref[0] = kv[:, width:].astype(BF16)


def _memory_kv(mem, mem_norm, w_kv):
    batch, mem_len, d_model = mem.shape
    width = w_kv.shape[1] // 2
    resident = pl.BlockSpec(memory_space=pltpu.VMEM)
    return pl.pallas_call(
        _memory_kv_kernel,
        grid=(batch,),
        in_specs=[
            pl.BlockSpec((1, mem_len, d_model), lambda b: (b, 0, 0)),
            resident,
            pl.BlockSpec(memory_space=pl.ANY),
        ],
        out_specs=[
            pl.BlockSpec((1, width, mem_len), lambda b: (b, 0, 0)),
            pl.BlockSpec((1, mem_len, width), lambda b: (b, 0, 0)),
        ],
        out_shape=[
            jax.ShapeDtypeStruct((batch, width, mem_len), BF16),
            jax.ShapeDtypeStruct((batch, mem_len, width), BF16),
        ],
        scratch_shapes=_stage_scratch(*w_kv.shape),
        compiler_params=pltpu.CompilerParams(
            dimension_semantics=("arbitrary",), vmem_limit_bytes=KV_VMEM_BYTES),
        name="memory_kv",
    )(mem, mem_norm.reshape(1, d_model), w_kv)


def _mix_rows(h, prev_rows, first_pos, w_in_ref, pool_w_ref, pool_scale_ref, ln_g_ref, ln_b_ref,
              sgu_w_ref, sgu_b_ref, kt_ref, v_ref, g_br_ref, y_ref, row0):
    rows_n = h.shape[0]
    width = pool_scale_ref.shape[-1]
    out_rows = pl.ds(row0, rows_n)

    pa = _dot(h, w_in_ref[:, 0:2 * width])
    xa, ga = pa[:, :width], pa[:, width:]
    rows = jnp.concatenate([prev_rows, xa], axis=0)
    seen = (first_pos + 1 + lax.broadcasted_iota(jnp.int32, (rows_n, 1), 0)).astype(F32)
    group_dim = width // len(POOL_WINDOWS)
    ya_groups = []
    for g, window in enumerate(POOL_WINDOWS):
        s = rows[:, g * group_dim:(g + 1) * group_dim]
        shift = 1
        while shift < window:
            s = s + pltpu.roll(s, shift, axis=0)
            shift *= 2
        xg = xa[:, g * group_dim:(g + 1) * group_dim]
        ya_groups.append(s[POOL_HALO:, :] / jnp.minimum(seen, float(window)) - xg)
    ya = jnp.concatenate(ya_groups, axis=-1) * pool_scale_ref[...] * _silu(ga)
    y_ref[0, out_rows, 0:width] = (_rms_scale(ya) * g_br_ref[:, 0:width]).astype(BF16)

    pb = _dot(h, w_in_ref[:, 2 * width:5 * width])
    u, vb, gb = pb[:, :width], pb[:, width:2 * width], pb[:, 2 * width:]
    vc = vb - jnp.mean(vb, axis=-1, keepdims=True)
    vn = vc * lax.rsqrt(jnp.mean(vc * vc, axis=-1, keepdims=True) + EPS)
    vn = (vn * ln_g_ref[...] + ln_b_ref[...]).astype(BF16)
    causal = (lax.broadcasted_iota(jnp.int32, (SGU_CHUNK, SGU_CHUNK), 0)
              >= lax.broadcasted_iota(jnp.int32, (SGU_CHUNK, SGU_CHUNK), 1))
    head_dim = width // SGU_HEADS
    sgu_bt = sgu_b_ref[...].T
    z_heads = []
    for hd in range(SGU_HEADS):
        w_hd = jnp.where(causal, sgu_w_ref[hd], 0.0).astype(BF16)
        bias = sgu_bt[:, hd:hd + 1]
        z_chunks = []
        for c in range(rows_n // SGU_CHUNK):
            v_blk = vn[c * SGU_CHUNK:(c + 1) * SGU_CHUNK, hd * head_dim:(hd + 1) * head_dim]
            z_chunks.append(_dot(w_hd, v_blk) + bias)
        z_heads.append(jnp.concatenate(z_chunks, axis=0))
    yb = u * jnp.concatenate(z_heads, axis=-1) * _silu(gb)
    y_ref[0, out_rows, width:2 * width] = (
        _rms_scale(yb) * g_br_ref[:, width:2 * width]).astype(BF16)

    pc = _dot(h, w_in_ref[:, 5 * width:7 * width])
    q, gc = pc[:, :width], pc[:, width:]
    attn_dim = width // XATTN_HEADS
    o_heads = []
    for hd in range(XATTN_HEADS):
        cols = slice(hd * attn_dim, (hd + 1) * attn_dim)
        s = _dot(q[:, cols].astype(BF16), kt_ref[0, cols, :])
        e = jnp.exp(s - jnp.max(s, axis=-1, keepdims=True))
        p = e / jnp.sum(e, axis=-1, keepdims=True)
        o_heads.append(_dot(p.astype(BF16), v_ref[0, :, cols]))
    yc = jnp.concatenate(o_heads, axis=-1) * _silu(gc)
    y_ref[0, out_rows, 2 * width:3 * width] = (
        _rms_scale(yc) * g_br_ref[:, 2 * width:3 * width]).astype(BF16)
    return xa[rows_n - POOL_HALO:, :]


def _mixers_kernel(x_ref, g_pre_ref, w_in_hbm, pool_w_ref, pool_scale_ref, ln_g_ref, ln_b_ref,
                   sgu_w_ref, sgu_b_ref, kt_ref, v_ref, g_br_ref, y_ref,
                   halo_ref, w_in_ref, stage_ref, sem):
    tile = x_ref.shape[1]
    seq_tile = pl.program_id(1)

    @pl.when((pl.program_id(0) == 0) & (seq_tile == 0))
    def _():
        _load_weight_bf16(w_in_hbm, w_in_ref, stage_ref, sem)
        group_dim = pool_w_ref.shape[-1]
        for g in range(len(POOL_WINDOWS)):
            cols = slice(g * group_dim, (g + 1) * group_dim)
            w_in_ref[:, cols] = _dot(w_in_ref[:, cols], pool_w_ref[g].astype(BF16)).astype(BF16)

    @pl.when(seq_tile == 0)
    def _():
        halo_ref[...] = jnp.zeros_like(halo_ref)

    prev_rows = halo_ref[...]
    for row0 in range(0, tile, MIXER_ROWS):
        h = (_rms_scale(x_ref[0, row0:row0 + MIXER_ROWS, :]) * g_pre_ref[...]).astype(BF16)
        prev_rows = _mix_rows(h, prev_rows, seq_tile * tile + row0, w_in_ref, pool_w_ref,
                              pool_scale_ref, ln_g_ref, ln_b_ref, sgu_w_ref, sgu_b_ref, kt_ref,
                              v_ref, g_br_ref, y_ref, row0)
    halo_ref[...] = prev_rows


def _mixers(x, g_pre, w_in, pool_w, pool_scale, ln_g, ln_b, sgu_w, sgu_b, kt, v, g_branch):
    batch, seq, d_model = x.shape
    width = pool_scale.shape[-1]
    mem_len = v.shape[1]
    resident = pl.BlockSpec(memory_space=pltpu.VMEM)
    return pl.pallas_call(
        _mixers_kernel,
        grid=(batch, seq // MIXER_TILE),
        in_specs=[
            pl.BlockSpec((1, MIXER_TILE, d_model), lambda b, j: (b, j, 0)),
            resident,
            pl.BlockSpec(memory_space=pl.ANY),
            resident,
            resident,
            resident,
            resident,
            resident,
            resident,
            pl.BlockSpec((1, width, mem_len), lambda b, j: (b, 0, 0)),
            pl.BlockSpec((1, mem_len, width), lambda b, j: (b, 0, 0)),
            resident,
        ],
        out_specs=pl.BlockSpec((1, MIXER_TILE, 3 * width), lambda b, j: (b, j, 0)),
        out_shape=jax.ShapeDtypeStruct((batch, seq, 3 * width), BF16),
        scratch_shapes=[pltpu.VMEM((POOL_HALO, width), F32)] + _stage_scratch(*w_in.shape),
        compiler_params=pltpu.CompilerParams(
            dimension_semantics=("arbitrary", "arbitrary"), vmem_limit_bytes=MIXER_VMEM_BYTES),
        name="mixers",
    )(x, g_pre, w_in, pool_w, pool_scale, ln_g, ln_b, sgu_w, sgu_b, kt, v, g_branch)


def _out_project_kernel(y_ref, x_ref, w_out_hbm, g_post_ref, o_ref, w_out_ref, stage_ref, sem):
    @pl.when(pl.program_id(0) == 0)
    def _():
        _load_weight_bf16(w_out_hbm, w_out_ref, stage_ref, sem)

    out = _dot(y_ref[...], w_out_ref[...])
    o_ref[...] = x_ref[...] + _rms_scale(out) * g_post_ref[...]


def _out_project(y, x, w_out, g_post):
    tokens, d_model = x.shape
    mix_width = y.shape[-1]
    return pl.pallas_call(
        _out_project_kernel,
        grid=(tokens // OUT_TILE,),
        in_specs=[
            pl.BlockSpec((OUT_TILE, mix_width), lambda i: (i, 0)),
            pl.BlockSpec((OUT_TILE, d_model), lambda i: (i, 0)),
            pl.BlockSpec(memory_space=pl.ANY),
            pl.BlockSpec(memory_space=pltpu.VMEM),
        ],
        out_specs=pl.BlockSpec((OUT_TILE, d_model), lambda i: (i, 0)),
        out_shape=jax.ShapeDtypeStruct((tokens, d_model), F32),
        scratch_shapes=_stage_scratch(*w_out.shape),
        compiler_params=pltpu.CompilerParams(
            dimension_semantics=("arbitrary",), vmem_limit_bytes=OUT_VMEM_BYTES),
        name="out_project",
    )(y, x, w_out, g_post)


def kernel(x, mem, norm_pre, w_in, pool_w, pool_scale, sgu_ln_g, sgu_ln_b, sgu_w, sgu_b,
           mem_norm, w_kv, branch_norm, w_out, norm_post):
    batch, seq, d_model = x.shape
    depth = w_in.shape[0]
    assert seq % MIXER_TILE == 0 and MIXER_TILE % MIXER_ROWS == 0
    assert MIXER_ROWS % SGU_CHUNK == 0 and MIXER_ROWS >= POOL_HALO
    assert (batch * seq) % OUT_TILE == 0
    row = lambda a: a.reshape(1, -1)
    for l in range(depth):
        kt, v = _memory_kv(mem, mem_norm, w_kv[l])
        y = _mixers(x, row(norm_pre[l]), w_in[l], pool_w[l], row(pool_scale[l]),
                    row(sgu_ln_g[l]), row(sgu_ln_b[l]), sgu_w[l], sgu_b[l],
                    kt, v, row(branch_norm[l]))
        x = _out_project(y.reshape(batch * seq, -1), x.reshape(batch * seq, d_model),
                         w_out[l], row(norm_post[l])).reshape(batch, seq, d_model)
    return x
```

```python
import math

import jax
import jax.numpy as jnp
from jax import lax
from jax.experimental import pallas as pl
from jax.experimental.pallas import tpu as pltpu

EPS = 1e-6
LOG2_E = 1.4426950408889634
POOL_WINDOWS = (2, 4, 8, 16)
POOL_HALO = 16
SGU_CHUNK = 128
SGU_HEADS = 8
XATTN_HEADS = 4

MIXER_TILE = 512
MIXER_ROWS = 256
OUT_TILE = 512
MIB = 1024 * 1024
MIXER_VMEM_BYTES = 62 * MIB
OUT_VMEM_BYTES = 62 * MIB
KV_VMEM_BYTES = 62 * MIB

STAGE_SLOTS = 14
STAGE_BYTES = 512 * 1024
BF16_SUBLANES = 16

F32 = jnp.float32
BF16 = jnp.bfloat16


def _rms_scale(v):
    return v * lax.rsqrt(jnp.mean(v * v, axis=-1, keepdims=True) + EPS)


def _silu(v):
    return v / (1.0 + jnp.exp2(v * -LOG2_E))


def _dot(a, b):
    return jnp.dot(a, b, preferred_element_type=F32)


def _stage_scratch(n_rows, n_cols):
    rows = max(BF16_SUBLANES, STAGE_BYTES // (4 * n_cols) // BF16_SUBLANES * BF16_SUBLANES)
    assert n_rows % rows == 0
    return [pltpu.VMEM((n_rows, n_cols), BF16),
            pltpu.VMEM((STAGE_SLOTS, rows, n_cols), F32),
            pltpu.SemaphoreType.DMA((STAGE_SLOTS,))]


def _load_weight_bf16(w_hbm, w_vmem, stage, sem):
    slots, rows = stage.shape[0], stage.shape[1]
    n_chunks = w_hbm.shape[0] // rows

    def chunk_copy(i, slot):
        return pltpu.make_async_copy(w_hbm.at[pl.ds(i * rows, rows), :], stage.at[slot], sem.at[slot])

    for i in range(min(slots - 1, n_chunks)):
        chunk_copy(i, i).start()

    def body(i, carry):
        ahead = i + slots - 1

        @pl.when(ahead < n_chunks)
        def _():
            chunk_copy(ahead, ahead % slots).start()

        slot = i % slots
        chunk_copy(i, slot).wait()
        w_vmem[pl.ds(pl.multiple_of(i * rows, rows), rows), :] = stage[slot].astype(BF16)
        return carry

    lax.fori_loop(0, n_chunks, body, 0)


def _memory_kv_kernel(mem_ref, g_ref, wkv_hbm, kt_ref, v_ref, wkv_ref, stage_ref, sem):
    @pl.when(pl.program_id(0) == 0)
    def _():
        _load_weight_bf16(wkv_hbm, wkv_ref, stage_ref, sem)

    width = v_ref.shape[-1]
    mem_n = (_rms_scale(mem_ref[0]) * g_ref[...]).astype(BF16)
    kv = _dot(mem_n, wkv_ref[...])
    k = kv[:, :width] * (1.0 / math.sqrt(width // XATTN_HEADS))
    kt_ref[0] = k.T.astype(BF16)
    v_ref[0] = kv[:, width:].astype(BF16)


def _memory_kv(mem, mem_norm, w_kv):
    batch, mem_len, d_model = mem.shape
    width = w_kv.shape[1] // 2
    resident = pl.BlockSpec(memory_space=pltpu.VMEM)
    return pl.pallas_call(
        _memory_kv_kernel,
        grid=(batch,),
        in_specs=[
            pl.BlockSpec((1, mem_len, d_model), lambda b: (b, 0, 0)),
            resident,
            pl.BlockSpec(memory_space=pl.ANY),
        ],
        out_specs=[
            pl.BlockSpec((1, width, mem_len), lambda b: (b, 0, 0)),
            pl.BlockSpec((1, mem_len, width), lambda b: (b, 0, 0)),
        ],
        out_shape=[
            jax.ShapeDtypeStruct((batch, width, mem_len), BF16),
            jax.ShapeDtypeStruct((batch, mem_len, width), BF16),
        ],
        scratch_shapes=_stage_scratch(*w_kv.shape),
        compiler_params=pltpu.CompilerParams(
            dimension_semantics=("arbitrary",), vmem_limit_bytes=KV_VMEM_BYTES),
        name="memory_kv",
    )(mem, mem_norm.reshape(1, d_model), w_kv)


def _mix_rows(h, prev_rows, first_pos, w_in_ref, pool_w_ref, pool_scale_ref, ln_g_ref, ln_b_ref,
              sgu_w_ref, sgu_b_ref, kt_ref, v_ref, g_br_ref, y_ref, row0):
    rows_n = h.shape[0]
    width = pool_scale_ref.shape[-1]
    out_rows = pl.ds(row0, rows_n)

    pa = _dot(h, w_in_ref[:, 0:2 * width])
    xa, ga = pa[:, :width], pa[:, width:]
    rows = jnp.concatenate([prev_rows, xa], axis=0)
    seen = (first_pos + 1 + lax.broadcasted_iota(jnp.int32, (rows_n, 1), 0)).astype(F32)
    group_dim = width // len(POOL_WINDOWS)
    ya_groups = []
    for g, window in enumerate(POOL_WINDOWS):
        s = rows[:, g * group_dim:(g + 1) * group_dim]
        shift = 1
        while shift < window:
            s = s + pltpu.roll(s, shift, axis=0)
            shift *= 2
        xg = xa[:, g * group_dim:(g + 1) * group_dim]
        ya_groups.append(s[POOL_HALO:, :] / jnp.minimum(seen, float(window)) - xg)
    ya = jnp.concatenate(ya_groups, axis=-1) * pool_scale_ref[...] * _silu(ga)
    y_ref[0, out_rows, 0:width] = (_rms_scale(ya) * g_br_ref[:, 0:width]).astype(BF16)

    pb = _dot(h, w_in_ref[:, 2 * width:5 * width])
    u, vb, gb = pb[:, :width], pb[:, width:2 * width], pb[:, 2 * width:]
    vc = vb - jnp.mean(vb, axis=-1, keepdims=True)
    vn = vc * lax.rsqrt(jnp.mean(vc * vc, axis=-1, keepdims=True) + EPS)
    vn = (vn * ln_g_ref[...] + ln_b_ref[...]).astype(BF16)
    causal = (lax.broadcasted_iota(jnp.int32, (SGU_CHUNK, SGU_CHUNK), 0)
              >= lax.broadcasted_iota(jnp.int32, (SGU_CHUNK, SGU_CHUNK), 1))
    head_dim = width // SGU_HEADS
    sgu_bt = sgu_b_ref[...].T
    z_heads = []
    for hd in range(SGU_HEADS):
        w_hd = jnp.where(causal, sgu_w_ref[hd], 0.0).astype(BF16)
        bias = sgu_bt[:, hd:hd + 1]
        z_chunks = []
        for c in range(rows_n // SGU_CHUNK):
            v_blk = vn[c * SGU_CHUNK:(c + 1) * SGU_CHUNK, hd * head_dim:(hd + 1) * head_dim]
            z_chunks.append(_dot(w_hd, v_blk) + bias)
        z_heads.append(jnp.concatenate(z_chunks, axis=0))
    yb = u * jnp.concatenate(z_heads, axis=-1) * _silu(gb)
    y_ref[0, out_rows, width:2 * width] = (
        _rms_scale(yb) * g_br_ref[:, width:2 * width]).astype(BF16)

    pc = _dot(h, w_in_ref[:, 5 * width:7 * width])
    q, gc = pc[:, :width], pc[:, width:]
    attn_dim = width // XATTN_HEADS
    o_heads = []
    for hd in range(XATTN_HEADS):
        cols = slice(hd * attn_dim, (hd + 1) * attn_dim)
        s = _dot(q[:, cols].astype(BF16), kt_ref[0, cols, :])
        e = jnp.exp(s - jnp.max(s, axis=-1, keepdims=True))
        p = e / jnp.sum(e, axis=-1, keepdims=True)
        o_heads.append(_dot(p.astype(BF16), v_ref[0, :, cols]))
    yc = jnp.concatenate(o_heads, axis=-1) * _silu(gc)
    y_ref[0, out_rows, 2 * width:3 * width] = (
        _rms_scale(yc) * g_br_ref[:, 2 * width:3 * width]).astype(BF16)
    return xa[rows_n - POOL_HALO:, :]


def _mixers_kernel(x_ref, g_pre_ref, w_in_hbm, pool_w_ref, pool_scale_ref, ln_g_ref, ln_b_ref,
                   sgu_w_ref, sgu_b_ref, kt_ref, v_ref, g_br_ref, y_ref,
                   halo_ref, w_in_ref, stage_ref, sem):
    tile = x_ref.shape[1]
    seq_tile = pl.program_id(1)

    @pl.when((pl.program_id(0) == 0) & (seq_tile == 0))
    def _():
        _load_weight_bf16(w_in_hbm, w_in_ref, stage_ref, sem)
        group_dim = pool_w_ref.shape[-1]
        for g in range(len(POOL_WINDOWS)):
            cols = slice(g * group_dim, (g + 1) * group_dim)
            w_in_ref[:, cols] = _dot(w_in_ref[:, cols], pool_w_ref[g].astype(BF16)).astype(BF16)

    @pl.when(seq_tile == 0)
    def _():
        halo_ref[...] = jnp.zeros_like(halo_ref)

    prev_rows = halo_ref[...]
    for row0 in range(0, tile, MIXER_ROWS):
        h = (_rms_scale(x_ref[0, row0:row0 + MIXER_ROWS, :]) * g_pre_ref[...]).astype(BF16)
        prev_rows = _mix_rows(h, prev_rows, seq_tile * tile + row0, w_in_ref, pool_w_ref,
                              pool_scale_ref, ln_g_ref, ln_b_ref, sgu_w_ref, sgu_b_ref, kt_ref,
                              v_ref, g_br_ref, y_ref, row0)
    halo_ref[...] = prev_rows


def _mixers(x, g_pre, w_in, pool_w, pool_scale, ln_g, ln_b, sgu_w, sgu_b, kt, v, g_branch):
    batch, seq, d_model = x.shape
    width = pool_scale.shape[-1]
    mem_len = v.shape[1]
    resident = pl.BlockSpec(memory_space=pltpu.VMEM)
    return pl.pallas_call(
        _mixers_kernel,
        grid=(batch, seq // MIXER_TILE),
        in_specs=[
            pl.BlockSpec((1, MIXER_TILE, d_model), lambda b, j: (b, j, 0)),
            resident,
            pl.BlockSpec(memory_space=pl.ANY),
            resident,
            resident,
            resident,
            resident,
            resident,
            resident,
            pl.BlockSpec((1, width, mem_len), lambda b, j: (b, 0, 0)),
            pl.BlockSpec((1, mem_len, width), lambda b, j: (b, 0, 0)),
            resident,
        ],
        out_specs=pl.BlockSpec((1, MIXER_TILE, 3 * width), lambda b, j: (b, j, 0)),
        out_shape=jax.ShapeDtypeStruct((batch, seq, 3 * width), BF16),
        scratch_shapes=[pltpu.VMEM((POOL_HALO, width), F32)] + _stage_scratch(*w_in.shape),
        compiler_params=pltpu.CompilerParams(
            dimension_semantics=("arbitrary", "arbitrary"), vmem_limit_bytes=MIXER_VMEM_BYTES),
        name="mixers",
    )(x, g_pre, w_in, pool_w, pool_scale, ln_g, ln_b, sgu_w, sgu_b, kt, v, g_branch)


def _out_project_kernel(y_ref, x_ref, w_out_hbm, g_post_ref, o_ref, w_out_ref, stage_ref, sem):
    @pl.when(pl.program_id(0) == 0)
    def _():
        _load_weight_bf16(w_out_hbm, w_out_ref, stage_ref, sem)

    out = _dot(y_ref[...], w_out_ref[...])
    o_ref[...] = x_ref[...] + _rms_scale(out) * g_post_ref[...]


def _out_project(y, x, w_out, g_post):
    tokens, d_model = x.shape
    mix_width = y.shape[-1]
    return pl.pallas_call(
        _out_project_kernel,
        grid=(tokens // OUT_TILE,),
        in_specs=[
            pl.BlockSpec((OUT_TILE, mix_width), lambda i: (i, 0)),
            pl.BlockSpec((OUT_TILE, d_model), lambda i: (i, 0)),
            pl.BlockSpec(memory_space=pl.ANY),
            pl.BlockSpec(memory_space=pltpu.VMEM),
        ],
        out_specs=pl.BlockSpec((OUT_TILE, d_model), lambda i: (i, 0)),
        out_shape=jax.ShapeDtypeStruct((tokens, d_model), F32),
        scratch_shapes=_stage_scratch(*w_out.shape),
        compiler_params=pltpu.CompilerParams(
            dimension_semantics=("arbitrary",), vmem_limit_bytes=OUT_VMEM_BYTES),
        name="out_project",
    )(y, x, w_out, g_post)


def kernel(x, mem, norm_pre, w_in, pool_w, pool_scale, sgu_ln_g, sgu_ln_b, sgu_w, sgu_b,
           mem_norm, w_kv, branch_norm, w_out, norm_post):
    batch, seq, d_model = x.shape
    depth = w_in.shape[0]
    assert seq % MIXER_TILE == 0 and MIXER_TILE % MIXER_ROWS == 0
    assert MIXER_ROWS % SGU_CHUNK == 0 and MIXER_ROWS >= POOL_HALO
    assert (batch * seq) % OUT_TILE == 0
    row = lambda a: a.reshape(1, -1)
    for l in range(depth):
        kt, v = _memory_kv(mem, mem_norm, w_kv[l])
        y = _mixers(x, row(norm_pre[l]), w_in[l], pool_w[l], row(pool_scale[l]),
                    row(sgu_ln_g[l]), row(sgu_ln_b[l]), sgu_w[l], sgu_b[l],
                    kt, v, row(branch_norm[l]))
        x = _out_project(y.reshape(batch * seq, -1), x.reshape(batch * seq, d_model),
                         w_out[l], row(norm_post[l])).reshape(batch, seq, d_model)
    return x
```

```python
import math

import jax
import jax.numpy as jnp
from jax import lax
from jax.experimental import pallas as pl
from jax.experimental.pallas import tpu as pltpu

EPS = 1e-6
LOG2_E = 1.4426950408889634
POOL_WINDOWS = (2, 4, 8, 16)
POOL_HALO = 16
SGU_CHUNK = 128
SGU_HEADS = 8
XATTN_HEADS = 4

MIXER_TILE = 512
MIXER_ROWS = 256
OUT_TILE = 512
MIB = 1024 * 1024
V7X_VMEM_BYTES = 64 * MIB
CALL_VMEM_BYTES = V7X_VMEM_BYTES - 2 * MIB

STAGE_SLOTS = 12
STAGE_BYTES = 512 * 1024
BF16_SUBLANES = 16

F32 = jnp.float32
BF16 = jnp.bfloat16


def _rms_scale(v):
    return v * lax.rsqrt(jnp.mean(v * v, axis=-1, keepdims=True) + EPS)


def _silu(v):
    return v / (1.0 + jnp.exp2(v * -LOG2_E))


def _dot(a, b):
    return jnp.dot(a, b, preferred_element_type=F32)


def _stage_scratch(n_rows, n_cols):
    rows = max(BF16_SUBLANES, STAGE_BYTES // (4 * n_cols) // BF16_SUBLANES * BF16_SUBLANES)
    assert n_rows % rows == 0
    return [pltpu.VMEM((n_rows, n_cols), BF16),
            pltpu.VMEM((STAGE_SLOTS, rows, n_cols), F32),
            pltpu.SemaphoreType.DMA((STAGE_SLOTS,))]


def _load_weight_bf16(w_hbm, w_vmem, stage, sem):
    slots, rows = stage.shape[0], stage.shape[1]
    n_chunks = w_hbm.shape[0] // rows

    def chunk_copy(i, slot):
        return pltpu.make_async_copy(w_hbm.at[pl.ds(i * rows, rows), :], stage.at[slot], sem.at[slot])

    for i in range(min(slots - 1, n_chunks)):
        chunk_copy(i, i).start()

    def body(i, carry):
        ahead = i + slots - 1

        @pl.when(ahead < n_chunks)
        def _():
            chunk_copy(ahead, ahead % slots).start()

        slot = i % slots
        chunk_copy(i, slot).wait()
        w_vmem[pl.ds(pl.multiple_of(i * rows, rows), rows), :] = stage[slot].astype(BF16)
        return carry

    lax.fori_loop(0, n_chunks, body, 0)


def _memory_kv_kernel(mem_ref, g_ref, wkv_hbm, kt_ref, v_ref, wkv_ref, stage_ref, sem):
    @pl.when(pl.program_id(0) == 0)
    def _():
        _load_weight_bf16(wkv_hbm, wkv_ref, stage_ref, sem)

    width = v_ref.shape[-1]
    mem_n = (_rms_scale(mem_ref[0]) * g_ref[...]).astype(BF16)
    kv = _dot(mem_n, wkv_ref[...])
    k = kv[:, :width] * (1.0 / math.sqrt(width // XATTN_HEADS))
    kt_ref[0] = k.T.astype(BF16)
    v_ref[0] = kv[:, width:].astype(BF16)


def _memory_kv(mem, mem_norm, w_kv):
    batch, mem_len, d_model = mem.shape
    width = w_kv.shape[1] // 2
    resident = pl.BlockSpec(memory_space=pltpu.VMEM)
    return pl.pallas_call(
        _memory_kv_kernel,
        grid=(batch,),
        in_specs=[
            pl.BlockSpec((1, mem_len, d_model), lambda b: (b, 0, 0)),
            resident,
            pl.BlockSpec(memory_space=pl.ANY),
        ],
        out_specs=[
            pl.BlockSpec((1, width, mem_len), lambda b: (b, 0, 0)),
            pl.BlockSpec((1, mem_len, width), lambda b: (b, 0, 0)),
        ],
        out_shape=[
            jax.ShapeDtypeStruct((batch, width, mem_len), BF16),
            jax.ShapeDtypeStruct((batch, mem_len, width), BF16),
        ],
        scratch_shapes=_stage_scratch(*w_kv.shape),
        compiler_params=pltpu.CompilerParams(
            dimension_semantics=("arbitrary",), vmem_limit_bytes=CALL_VMEM_BYTES),
        name="memory_kv",
    )(mem, mem_norm.reshape(1, d_model), w_kv)


def _mix_rows(h, prev_rows, first_pos, w_in_ref, pool_scale_ref, ln_g_ref, ln_b_ref,
              sgu_w_ref, sgu_b_ref, kt_ref, v_ref, g_br_ref, y_ref, row0):
    rows_n = h.shape[0]
    width = pool_scale_ref.shape[-1]
    out_rows = pl.ds(row0, rows_n)

    pa = _dot(h, w_in_ref[:, 0:2 * width])
    xa, ga = pa[:, :width], pa[:, width:]
    rows = jnp.concatenate([prev_rows, xa], axis=0)
    seen = (first_pos + 1 + lax.broadcasted_iota(jnp.int32, (rows_n, 1), 0)).astype(F32)
    group_dim = width // len(POOL_WINDOWS)
    ya_groups = []
    for g, window in enumerate(POOL_WINDOWS):
        s = rows[:, g * group_dim:(g + 1) * group_dim]
        shift = 1
        while shift < window:
            s = s + pltpu.roll(s, shift, axis=0)
            shift *= 2
        xg = xa[:, g * group_dim:(g + 1) * group_dim]
        ya_groups.append(s[POOL_HALO:, :] / jnp.minimum(seen, float(window)) - xg)
    ya = jnp.concatenate(ya_groups, axis=-1) * pool_scale_ref[...] * _silu(ga)
    y_ref[0, out_rows, 0:width] = (_rms_scale(ya) * g_br_ref[:, 0:width]).astype(BF16)

    pb = _dot(h, w_in_ref[:, 2 * width:5 * width])
    u, vb, gb = pb[:, :width], pb[:, width:2 * width], pb[:, 2 * width:]
    vc = vb - jnp.mean(vb, axis=-1, keepdims=True)
    vn = vc * lax.rsqrt(jnp.mean(vc * vc, axis=-1, keepdims=True) + EPS)
    vn = (vn * ln_g_ref[...] + ln_b_ref[...]).astype(BF16)
    causal = (lax.broadcasted_iota(jnp.int32, (SGU_CHUNK, SGU_CHUNK), 0)
              >= lax.broadcasted_iota(jnp.int32, (SGU_CHUNK, SGU_CHUNK), 1))
    head_dim = width // SGU_HEADS
    sgu_bt = sgu_b_ref[...].T
    z_heads = []
    for hd in range(SGU_HEADS):
        w_hd = jnp.where(causal, sgu_w_ref[hd], 0.0).astype(BF16)
        bias = sgu_bt[:, hd:hd + 1]
        z_chunks = []
        for c in range(rows_n // SGU_CHUNK):
            v_blk = vn[c * SGU_CHUNK:(c + 1) * SGU_CHUNK, hd * head_dim:(hd + 1) * head_dim]
            z_chunks.append(_dot(w_hd, v_blk) + bias)
        z_heads.append(jnp.concatenate(z_chunks, axis=0))
    yb = u * jnp.concatenate(z_heads, axis=-1) * _silu(gb)
    y_ref[0, out_rows, width:2 * width] = (
        _rms_scale(yb) * g_br_ref[:, width:2 * width]).astype(BF16)

    pc = _dot(h, w_in_ref[:, 5 * width:7 * width])
    q, gc = pc[:, :width], pc[:, width:]
    attn_dim = width // XATTN_HEADS
    o_heads = []
    for hd in range(XATTN_HEADS):
        cols = slice(hd * attn_dim, (hd + 1) * attn_dim)
        s = _dot(q[:, cols].astype(BF16), kt_ref[0, cols, :])
        e = jnp.exp(s - jnp.max(s, axis=-1, keepdims=True))
        p = e / jnp.sum(e, axis=-1, keepdims=True)
        o_heads.append(_dot(p.astype(BF16), v_ref[0, :, cols]))
    yc = jnp.concatenate(o_heads, axis=-1) * _silu(gc)
    y_ref[0, out_rows, 2 * width:3 * width] = (
        _rms_scale(yc) * g_br_ref[:, 2 * width:3 * width]).astype(BF16)
    return xa[rows_n - POOL_HALO:, :]


def _mixers_kernel(x_ref, g_pre_ref, w_in_hbm, pool_w_ref, pool_scale_ref, ln_g_ref, ln_b_ref,
                   sgu_w_ref, sgu_b_ref, kt_ref, v_ref, g_br_ref, y_ref,
                   halo_ref, w_in_ref, stage_ref, sem):
    tile = x_ref.shape[1]
    seq_tile = pl.program_id(1)

    @pl.when((pl.program_id(0) == 0) & (seq_tile == 0))
    def _():
        _load_weight_bf16(w_in_hbm, w_in_ref, stage_ref, sem)
        group_dim = pool_w_ref.shape[-1]
        for g in range(len(POOL_WINDOWS)):
            cols = slice(g * group_dim, (g + 1) * group_dim)
            w_in_ref[:, cols] = _dot(w_in_ref[:, cols], pool_w_ref[g].astype(BF16)).astype(BF16)

    @pl.when(seq_tile == 0)
    def _():
        halo_ref[...] = jnp.zeros_like(halo_ref)

    prev_rows = halo_ref[...]
    for row0 in range(0, tile, MIXER_ROWS):
        h = (_rms_scale(x_ref[0, row0:row0 + MIXER_ROWS, :]) * g_pre_ref[...]).astype(BF16)
        prev_rows = _mix_rows(h, prev_rows, seq_tile * tile + row0, w_in_ref, pool_scale_ref,
                              ln_g_ref, ln_b_ref, sgu_w_ref, sgu_b_ref, kt_ref, v_ref, g_br_ref,
                              y_ref, row0)
    halo_ref[...] = prev_rows


def _mixers(x, g_pre, w_in, pool_w, pool_scale, ln_g, ln_b, sgu_w, sgu_b, kt, v, g_branch):
    batch, seq, d_model = x.shape
    width = pool_scale.shape[-1]
    mem_len = v.shape[1]
    resident = pl.BlockSpec(memory_space=pltpu.VMEM)
    return pl.pallas_call(
        _mixers_kernel,
        grid=(batch, seq // MIXER_TILE),
        in_specs=[
            pl.BlockSpec((1, MIXER_TILE, d_model), lambda b, j: (b, j, 0)),
            resident,
            pl.BlockSpec(memory_space=pl.ANY),
            resident,
            resident,
            resident,
            resident,
            resident,
            resident,
            pl.BlockSpec((1, width, mem_len), lambda b, j: (b, 0, 0)),
            pl.BlockSpec((1, mem_len, width), lambda b, j: (b, 0, 0)),
            resident,
        ],
        out_specs=pl.BlockSpec((1, MIXER_TILE, 3 * width), lambda b, j: (b, j, 0)),
        out_shape=jax.ShapeDtypeStruct((batch, seq, 3 * width), BF16),
        scratch_shapes=[pltpu.VMEM((POOL_HALO, width), F32)] + _stage_scratch(*w_in.shape),
        compiler_params=pltpu.CompilerParams(
            dimension_semantics=("arbitrary", "arbitrary"), vmem_limit_bytes=CALL_VMEM_BYTES),
        name="mixers",
    )(x, g_pre, w_in, pool_w, pool_scale, ln_g, ln_b, sgu_w, sgu_b, kt, v, g_branch)


def _out_project_kernel(y_ref, x_ref, w_out_hbm, g_post_ref, o_ref, w_out_ref, stage_ref, sem):
    @pl.when(pl.program_id(0) == 0)
    def _():
        _load_weight_bf16(w_out_hbm, w_out_ref, stage_ref, sem)

    out = _dot(y_ref[...], w_out_ref[...])
    o_ref[...] = x_ref[...] + _rms_scale(out) * g_post_ref[...]


def _out_project(y, x, w_out, g_post):
    tokens, d_model = x.shape
    mix_width = y.shape[-1]
    return pl.pallas_call(
        _out_project_kernel,
        grid=(tokens // OUT_TILE,),
        in_specs=[
            pl.BlockSpec((OUT_TILE, mix_width), lambda i: (i, 0)),
            pl.BlockSpec((OUT_TILE, d_model), lambda i: (i, 0)),
            pl.BlockSpec(memory_space=pl.ANY),
            pl.BlockSpec(memory_space=pltpu.VMEM),
        ],
        out_specs=pl.BlockSpec((OUT_TILE, d_model), lambda i: (i, 0)),
        out_shape=jax.ShapeDtypeStruct((tokens, d_model), F32),
        scratch_shapes=_stage_scratch(*w_out.shape),
        compiler_params=pltpu.CompilerParams(
            dimension_semantics=("arbitrary",), vmem_limit_bytes=CALL_VMEM_BYTES),
        name="out_project",
    )(y, x, w_out, g_post)


def kernel(x, mem, norm_pre, w_in, pool_w, pool_scale, sgu_ln_g, sgu_ln_b, sgu_w, sgu_b,
           mem_norm, w_kv, branch_norm, w_out, norm_post):
    batch, seq, d_model = x.shape
    depth = w_in.shape[0]
    assert seq % MIXER_TILE == 0 and MIXER_TILE % MIXER_ROWS == 0
    assert MIXER_ROWS % SGU_CHUNK == 0 and MIXER_ROWS >= POOL_HALO
    assert (batch * seq) % OUT_TILE == 0
    row = lambda a: a.reshape(1, -1)
    for l in range(depth):
        kt, v = _memory_kv(mem, mem_norm, w_kv[l])
        y = _mixers(x, row(norm_pre[l]), w_in[l], pool_w[l], row(pool_scale[l]),
                    row(sgu_ln_g[l]), row(sgu_ln_b[l]), sgu_w[l], sgu_b[l],
                    kt, v, row(branch_norm[l]))
        x = _out_project(y.reshape(batch * seq, -1), x.reshape(batch * seq, d_model),
                         w_out[l], row(norm_post[l])).reshape(batch, seq, d_model)
    return x
```

```python
import math

import jax
import jax.numpy as jnp
from jax import lax
from jax.experimental import pallas as pl
from jax.experimental.pallas import tpu as pltpu

EPS = 1e-6
LOG2_E = 1.4426950408889634
POOL_WINDOWS = (2, 4, 8, 16)
POOL_HALO = 16
SGU_CHUNK = 128
SGU_HEADS = 8
XATTN_HEADS = 4

MIXER_TILE = 512
MIXER_ROWS = 256
OUT_TILE = 512
MIB = 1024 * 1024
V7X_VMEM_BYTES = 64 * MIB
CALL_VMEM_BYTES = V7X_VMEM_BYTES - 2 * MIB

STAGE_SLOTS = 12
STAGE_BYTES = 512 * 1024
BF16_SUBLANES = 16
MXU_K_TILE = 256

F32 = jnp.float32
BF16 = jnp.bfloat16


def _rms_scale(v):
    return v * lax.rsqrt(jnp.mean(v * v, axis=-1, keepdims=True) + EPS)


def _silu(v):
    return v / (1.0 + jnp.exp2(v * -LOG2_E))


def _dot(a, b):
    return jnp.dot(a, b, preferred_element_type=F32)


def _stage_scratch(n_rows, n_cols):
    rows = max(BF16_SUBLANES, STAGE_BYTES // (4 * n_cols) // BF16_SUBLANES * BF16_SUBLANES)
    assert n_rows % rows == 0
    return [pltpu.VMEM((n_rows, n_cols), BF16),
            pltpu.VMEM((STAGE_SLOTS, rows, n_cols), F32),
            pltpu.SemaphoreType.DMA((STAGE_SLOTS,))]


def _load_weight_bf16(w_hbm, w_vmem, stage, sem):
    slots, rows = stage.shape[0], stage.shape[1]
    n_chunks = w_hbm.shape[0] // rows

    def chunk_copy(i, slot):
        return pltpu.make_async_copy(w_hbm.at[pl.ds(i * rows, rows), :], stage.at[slot], sem.at[slot])

    for i in range(min(slots - 1, n_chunks)):
        chunk_copy(i, i).start()

    def body(i, carry):
        ahead = i + slots - 1

        @pl.when(ahead < n_chunks)
        def _():
            chunk_copy(ahead, ahead % slots).start()

        slot = i % slots
        chunk_copy(i, slot).wait()
        w_vmem[pl.ds(pl.multiple_of(i * rows, rows), rows), :] = stage[slot].astype(BF16)
        return carry

    lax.fori_loop(0, n_chunks, body, 0)


def _load_weight_and_first_product(w_hbm, w_vmem, stage, sem, lhs):
    slots, rows = stage.shape[0], stage.shape[1]
    n_chunks = w_hbm.shape[0] // rows
    chunks_per_tile = MXU_K_TILE // rows
    assert MXU_K_TILE % rows == 0 and n_chunks % chunks_per_tile == 0

    def chunk_copy(i):
        return pltpu.make_async_copy(
            w_hbm.at[pl.ds(i * rows, rows), :], stage.at[i % slots], sem.at[i % slots])

    for i in range(min(slots - 1, n_chunks)):
        chunk_copy(i).start()
    acc = None
    for tile in range(n_chunks // chunks_per_tile):
        for i in range(tile * chunks_per_tile, (tile + 1) * chunks_per_tile):
            if i + slots - 1 < n_chunks:
                chunk_copy(i + slots - 1).start()
            chunk_copy(i).wait()
            w_vmem[i * rows:(i + 1) * rows, :] = stage[i % slots].astype(BF16)
        k_rows = slice(tile * MXU_K_TILE, (tile + 1) * MXU_K_TILE)
        part = _dot(lhs[:, k_rows], w_vmem[k_rows, :])
        acc = part if acc is None else acc + part
    return acc


def _memory_kv_kernel(mem_ref, g_ref, wkv_hbm, kt_ref, v_ref, wkv_ref, stage_ref, sem):
    width = v_ref.shape[-1]
    mem_n = (_rms_scale(mem_ref[0]) * g_ref[...]).astype(BF16)

    def write(kv):
        k = kv[:, :width] * (1.0 / math.sqrt(width // XATTN_HEADS))
        kt_ref[0] = k.T.astype(BF16)
        v_ref[0] = kv[:, width:].astype(BF16)

    @pl.when(pl.program_id(0) == 0)
    def _():
        write(_load_weight_and_first_product(wkv_hbm, wkv_ref, stage_ref, sem, mem_n))

    @pl.when(pl.program_id(0) != 0)
    def _():
        write(_dot(mem_n, wkv_ref[...]))


def _memory_kv(mem, mem_norm, w_kv):
    batch, mem_len, d_model = mem.shape
    width = w_kv.shape[1] // 2
    resident = pl.BlockSpec(memory_space=pltpu.VMEM)
    return pl.pallas_call(
        _memory_kv_kernel,
        grid=(batch,),
        in_specs=[
            pl.BlockSpec((1, mem_len, d_model), lambda b: (b, 0, 0)),
            resident,
            pl.BlockSpec(memory_space=pl.ANY),
        ],
        out_specs=[
            pl.BlockSpec((1, width, mem_len), lambda b: (b, 0, 0)),
            pl.BlockSpec((1, mem_len, width), lambda b: (b, 0, 0)),
        ],
        out_shape=[
            jax.ShapeDtypeStruct((batch, width, mem_len), BF16),
            jax.ShapeDtypeStruct((batch, mem_len, width), BF16),
        ],
        scratch_shapes=_stage_scratch(*w_kv.shape),
        compiler_params=pltpu.CompilerParams(
            dimension_semantics=("arbitrary",), vmem_limit_bytes=CALL_VMEM_BYTES),
        name="memory_kv",
    )(mem, mem_norm.reshape(1, d_model), w_kv)


def _mix_rows(h, prev_rows, first_pos, w_in_ref, pool_scale_ref, ln_g_ref, ln_b_ref,
              sgu_w_ref, sgu_b_ref, kt_ref, v_ref, g_br_ref, y_ref, row0):
    rows_n = h.shape[0]
    width = pool_scale_ref.shape[-1]
    out_rows = pl.ds(row0, rows_n)

    pa = _dot(h, w_in_ref[:, 0:2 * width])
    xa, ga = pa[:, :width], pa[:, width:]
    rows = jnp.concatenate([prev_rows, xa], axis=0)
    seen = (first_pos + 1 + lax.broadcasted_iota(jnp.int32, (rows_n, 1), 0)).astype(F32)
    group_dim = width // len(POOL_WINDOWS)
    ya_groups = []
    for g, window in enumerate(POOL_WINDOWS):
        s = rows[:, g * group_dim:(g + 1) * group_dim]
        shift = 1
        while shift < window:
            s = s + pltpu.roll(s, shift, axis=0)
            shift *= 2
        xg = xa[:, g * group_dim:(g + 1) * group_dim]
        ya_groups.append(s[POOL_HALO:, :] / jnp.minimum(seen, float(window)) - xg)
    ya = jnp.concatenate(ya_groups, axis=-1) * pool_scale_ref[...] * _silu(ga)
    y_ref[0, out_rows, 0:width] = (_rms_scale(ya) * g_br_ref[:, 0:width]).astype(BF16)

    pb = _dot(h, w_in_ref[:, 2 * width:5 * width])
    u, vb, gb = pb[:, :width], pb[:, width:2 * width], pb[:, 2 * width:]
    vc = vb - jnp.mean(vb, axis=-1, keepdims=True)
    vn = vc * lax.rsqrt(jnp.mean(vc * vc, axis=-1, keepdims=True) + EPS)
    vn = (vn * ln_g_ref[...] + ln_b_ref[...]).astype(BF16)
    causal = (lax.broadcasted_iota(jnp.int32, (SGU_CHUNK, SGU_CHUNK), 0)
              >= lax.broadcasted_iota(jnp.int32, (SGU_CHUNK, SGU_CHUNK), 1))
    head_dim = width // SGU_HEADS
    sgu_bt = sgu_b_ref[...].T
    z_heads = []
    for hd in range(SGU_HEADS):
        w_hd = jnp.where(causal, sgu_w_ref[hd], 0.0).astype(BF16)
        bias = sgu_bt[:, hd:hd + 1]
        z_chunks = []
        for c in range(rows_n // SGU_CHUNK):
            v_blk = vn[c * SGU_CHUNK:(c + 1) * SGU_CHUNK, hd * head_dim:(hd + 1) * head_dim]
            z_chunks.append(_dot(w_hd, v_blk) + bias)
        z_heads.append(jnp.concatenate(z_chunks, axis=0))
    yb = u * jnp.concatenate(z_heads, axis=-1) * _silu(gb)
    y_ref[0, out_rows, width:2 * width] = (
        _rms_scale(yb) * g_br_ref[:, width:2 * width]).astype(BF16)

    pc = _dot(h, w_in_ref[:, 5 * width:7 * width])
    q, gc = pc[:, :width], pc[:, width:]
    attn_dim = width // XATTN_HEADS
    o_heads = []
    for hd in range(XATTN_HEADS):
        cols = slice(hd * attn_dim, (hd + 1) * attn_dim)
        s = _dot(q[:, cols].astype(BF16), kt_ref[0, cols, :])
        e = jnp.exp(s - jnp.max(s, axis=-1, keepdims=True))
        p = e / jnp.sum(e, axis=-1, keepdims=True)
        o_heads.append(_dot(p.astype(BF16), v_ref[0, :, cols]))
    yc = jnp.concatenate(o_heads, axis=-1) * _silu(gc)
    y_ref[0, out_rows, 2 * width:3 * width] = (
        _rms_scale(yc) * g_br_ref[:, 2 * width:3 * width]).astype(BF16)
    return xa[rows_n - POOL_HALO:, :]


def _mixers_kernel(x_ref, g_pre_ref, w_in_hbm, pool_w_ref, pool_scale_ref, ln_g_ref, ln_b_ref,
                   sgu_w_ref, sgu_b_ref, kt_ref, v_ref, g_br_ref, y_ref,
                   halo_ref, w_in_ref, stage_ref, sem):
    tile = x_ref.shape[1]
    seq_tile = pl.program_id(1)

    @pl.when((pl.program_id(0) == 0) & (seq_tile == 0))
    def _():
        _load_weight_bf16(w_in_hbm, w_in_ref, stage_ref, sem)
        group_dim = pool_w_ref.shape[-1]
        for g in range(len(POOL_WINDOWS)):
            cols = slice(g * group_dim, (g + 1) * group_dim)
            w_in_ref[:, cols] = _dot(w_in_ref[:, cols], pool_w_ref[g].astype(BF16)).astype(BF16)

    @pl.when(seq_tile == 0)
    def _():
        halo_ref[...] = jnp.zeros_like(halo_ref)

    prev_rows = halo_ref[...]
    for row0 in range(0, tile, MIXER_ROWS):
        h = (_rms_scale(x_ref[0, row0:row0 + MIXER_ROWS, :]) * g_pre_ref[...]).astype(BF16)
        prev_rows = _mix_rows(h, prev_rows, seq_tile * tile + row0, w_in_ref, pool_scale_ref,
                              ln_g_ref, ln_b_ref, sgu_w_ref, sgu_b_ref, kt_ref, v_ref, g_br_ref,
                              y_ref, row0)
    halo_ref[...] = prev_rows


def _mixers(x, g_pre, w_in, pool_w, pool_scale, ln_g, ln_b, sgu_w, sgu_b, kt, v, g_branch):
    batch, seq, d_model = x.shape
    width = pool_scale.shape[-1]
    mem_len = v.shape[1]
    resident = pl.BlockSpec(memory_space=pltpu.VMEM)
    return pl.pallas_call(
        _mixers_kernel,
        grid=(batch, seq // MIXER_TILE),
        in_specs=[
            pl.BlockSpec((1, MIXER_TILE, d_model), lambda b, j: (b, j, 0)),
            resident,
            pl.BlockSpec(memory_space=pl.ANY),
            resident,
            resident,
            resident,
            resident,
            resident,
            resident,
            pl.BlockSpec((1, width, mem_len), lambda b, j: (b, 0, 0)),
            pl.BlockSpec((1, mem_len, width), lambda b, j: (b, 0, 0)),
            resident,
        ],
        out_specs=pl.BlockSpec((1, MIXER_TILE, 3 * width), lambda b, j: (b, j, 0)),
        out_shape=jax.ShapeDtypeStruct((batch, seq, 3 * width), BF16),
        scratch_shapes=[pltpu.VMEM((POOL_HALO, width), F32)] + _stage_scratch(*w_in.shape),
        compiler_params=pltpu.CompilerParams(
            dimension_semantics=("arbitrary", "arbitrary"), vmem_limit_bytes=CALL_VMEM_BYTES),
        name="mixers",
    )(x, g_pre, w_in, pool_w, pool_scale, ln_g, ln_b, sgu_w, sgu_b, kt, v, g_branch)


def _out_project_kernel(y_ref, x_ref, w_out_hbm, g_post_ref, o_ref, w_out_ref, stage_ref, sem):
    @pl.when(pl.program_id(0) == 0)
    def _():
        _load_weight_bf16(w_out_hbm, w_out_ref, stage_ref, sem)

    out = _dot(y_ref[...], w_out_ref[...])
    o_ref[...] = x_ref[...] + _rms_scale(out) * g_post_ref[...]


def _out_project(y, x, w_out, g_post):
    tokens, d_model = x.shape
    mix_width = y.shape[-1]
    return pl.pallas_call(
        _out_project_kernel,
        grid=(tokens // OUT_TILE,),
        in_specs=[
            pl.BlockSpec((OUT_TILE, mix_width), lambda i: (i, 0)),
            pl.BlockSpec((OUT_TILE, d_model), lambda i: (i, 0)),
            pl.BlockSpec(memory_space=pl.ANY),
            pl.BlockSpec(memory_space=pltpu.VMEM),
        ],
        out_specs=pl.BlockSpec((OUT_TILE, d_model), lambda i: (i, 0)),
        out_shape=jax.ShapeDtypeStruct((tokens, d_model), F32),
        scratch_shapes=_stage_scratch(*w_out.shape),
        compiler_params=pltpu.CompilerParams(
            dimension_semantics=("arbitrary",), vmem_limit_bytes=CALL_VMEM_BYTES),
        name="out_project",
    )(y, x, w_out, g_post)


def kernel(x, mem, norm_pre, w_in, pool_w, pool_scale, sgu_ln_g, sgu_ln_b, sgu_w, sgu_b,
           mem_norm, w_kv, branch_norm, w_out, norm_post):
    batch, seq, d_model = x.shape
    depth = w_in.shape[0]
    assert seq % MIXER_TILE == 0 and MIXER_TILE % MIXER_ROWS == 0
    assert MIXER_ROWS % SGU_CHUNK == 0 and MIXER_ROWS >= POOL_HALO
    assert (batch * seq) % OUT_TILE == 0
    row = lambda a: a.reshape(1, -1)
    for l in range(depth):
        kt, v = _memory_kv(mem, mem_norm, w_kv[l])
        y = _mixers(x, row(norm_pre[l]), w_in[l], pool_w[l], row(pool_scale[l]),
                    row(sgu_ln_g[l]), row(sgu_ln_b[l]), sgu_w[l], sgu_b[l],
                    kt, v, row(branch_norm[l]))
        x = _out_project(y.reshape(batch * seq, -1), x.reshape(batch * seq, d_model),
                         w_out[l], row(norm_post[l])).reshape(batch, seq, d_model)
    return x
```

```python
import math

import jax
import jax.numpy as jnp
from jax import lax
from jax.experimental import pallas as pl
from jax.experimental.pallas import tpu as pltpu

EPS = 1e-6
LOG2_E = 1.4426950408889634
POOL_WINDOWS = (2, 4, 8, 16)
POOL_HALO = 16
SGU_CHUNK = 128
SGU_HEADS = 8
XATTN_HEADS = 4

MIXER_TILE = 512
MIXER_ROWS = 256
OUT_TILE = 512
MIB = 1024 * 1024
V7X_VMEM_BYTES = 64 * MIB
CALL_VMEM_BYTES = V7X_VMEM_BYTES - 2 * MIB

STAGE_SLOTS = 12
STAGE_BYTES = 512 * 1024
BF16_SUBLANES = 16
DMA_QUEUES = 2

F32 = jnp.float32
BF16 = jnp.bfloat16


def _rms_scale(v):
    return v * lax.rsqrt(jnp.mean(v * v, axis=-1, keepdims=True) + EPS)


def _silu(v):
    return v / (1.0 + jnp.exp2(v * -LOG2_E))


def _dot(a, b):
    return jnp.dot(a, b, preferred_element_type=F32)


def _stage_scratch(n_rows, n_cols):
    rows = max(BF16_SUBLANES, STAGE_BYTES // (4 * n_cols) // BF16_SUBLANES * BF16_SUBLANES)
    assert n_rows % rows == 0
    return [pltpu.VMEM((n_rows, n_cols), BF16),
            pltpu.VMEM((STAGE_SLOTS, rows, n_cols), F32),
            pltpu.SemaphoreType.DMA((STAGE_SLOTS,))]


def _load_weight_bf16(w_hbm, w_vmem, stage, sem):
    slots, rows = stage.shape[0], stage.shape[1]
    n_chunks = w_hbm.shape[0] // rows

    def chunk_copy(i, slot):
        return pltpu.make_async_copy(w_hbm.at[pl.ds(i * rows, rows), :], stage.at[slot], sem.at[slot])

    assert n_chunks % DMA_QUEUES == 0
    for i in range(min(slots - 1, n_chunks)):
        chunk_copy(i, i).start(priority=i % DMA_QUEUES)

    def body(group, carry):
        for lane in range(DMA_QUEUES):
            i = group * DMA_QUEUES + lane
            ahead = i + slots - 1

            @pl.when(ahead < n_chunks)
            def _(ahead=ahead, lane=lane):
                chunk_copy(ahead, ahead % slots).start(priority=(lane + slots - 1) % DMA_QUEUES)

            slot = i % slots
            chunk_copy(i, slot).wait()
            w_vmem[pl.ds(pl.multiple_of(i * rows, rows), rows), :] = stage[slot].astype(BF16)
        return carry

    lax.fori_loop(0, n_chunks // DMA_QUEUES, body, 0)


def _memory_kv_kernel(mem_ref, g_ref, wkv_hbm, kt_ref, v_ref, wkv_ref, stage_ref, sem):
    @pl.when(pl.program_id(0) == 0)
    def _():
        _load_weight_bf16(wkv_hbm, wkv_ref, stage_ref, sem)

    width = v_ref.shape[-1]
    mem_n = (_rms_scale(mem_ref[0]) * g_ref[...]).astype(BF16)
    kv = _dot(mem_n, wkv_ref[...])
    k = kv[:, :width] * (1.0 / math.sqrt(width // XATTN_HEADS))
    kt_ref[0] = k.T.astype(BF16)
    v_ref[0] = kv[:, width:].astype(BF16)


def _memory_kv(mem, mem_norm, w_kv):
    batch, mem_len, d_model = mem.shape
    width = w_kv.shape[1] // 2
    resident = pl.BlockSpec(memory_space=pltpu.VMEM)
    return pl.pallas_call(
        _memory_kv_kernel,
        grid=(batch,),
        in_specs=[
            pl.BlockSpec((1, mem_len, d_model), lambda b: (b, 0, 0)),
            resident,
            pl.BlockSpec(memory_space=pl.ANY),
        ],
        out_specs=[
            pl.BlockSpec((1, width, mem_len), lambda b: (b, 0, 0)),
            pl.BlockSpec((1, mem_len, width), lambda b: (b, 0, 0)),
        ],
        out_shape=[
            jax.ShapeDtypeStruct((batch, width, mem_len), BF16),
            jax.ShapeDtypeStruct((batch, mem_len, width), BF16),
        ],
        scratch_shapes=_stage_scratch(*w_kv.shape),
        compiler_params=pltpu.CompilerParams(
            dimension_semantics=("arbitrary",), vmem_limit_bytes=CALL_VMEM_BYTES),
        name="memory_kv",
    )(mem, mem_norm.reshape(1, d_model), w_kv)


def _mix_rows(h, prev_rows, first_pos, w_in_ref, pool_scale_ref, ln_g_ref, ln_b_ref,
              sgu_w_ref, sgu_b_ref, kt_ref, v_ref, g_br_ref, y_ref, row0):
    rows_n = h.shape[0]
    width = pool_scale_ref.shape[-1]
    out_rows = pl.ds(row0, rows_n)

    pa = _dot(h, w_in_ref[:, 0:2 * width])
    xa, ga = pa[:, :width], pa[:, width:]
    rows = jnp.concatenate([prev_rows, xa], axis=0)
    seen = (first_pos + 1 + lax.broadcasted_iota(jnp.int32, (rows_n, 1), 0)).astype(F32)
    group_dim = width // len(POOL_WINDOWS)
    ya_groups = []
    for g, window in enumerate(POOL_WINDOWS):
        s = rows[:, g * group_dim:(g + 1) * group_dim]
        shift = 1
        while shift < window:
            s = s + pltpu.roll(s, shift, axis=0)
            shift *= 2
        xg = xa[:, g * group_dim:(g + 1) * group_dim]
        ya_groups.append(s[POOL_HALO:, :] / jnp.minimum(seen, float(window)) - xg)
    ya = jnp.concatenate(ya_groups, axis=-1) * pool_scale_ref[...] * _silu(ga)
    y_ref[0, out_rows, 0:width] = (_rms_scale(ya) * g_br_ref[:, 0:width]).astype(BF16)

    pb = _dot(h, w_in_ref[:, 2 * width:5 * width])
    u, vb, gb = pb[:, :width], pb[:, width:2 * width], pb[:, 2 * width:]
    vc = vb - jnp.mean(vb, axis=-1, keepdims=True)
    vn = vc * lax.rsqrt(jnp.mean(vc * vc, axis=-1, keepdims=True) + EPS)
    vn = (vn * ln_g_ref[...] + ln_b_ref[...]).astype(BF16)
    causal = (lax.broadcasted_iota(jnp.int32, (SGU_CHUNK, SGU_CHUNK), 0)
              >= lax.broadcasted_iota(jnp.int32, (SGU_CHUNK, SGU_CHUNK), 1))
    head_dim = width // SGU_HEADS
    sgu_bt = sgu_b_ref[...].T
    z_heads = []
    for hd in range(SGU_HEADS):
        w_hd = jnp.where(causal, sgu_w_ref[hd], 0.0).astype(BF16)
        bias = sgu_bt[:, hd:hd + 1]
        z_chunks = []
        for c in range(rows_n // SGU_CHUNK):
            v_blk = vn[c * SGU_CHUNK:(c + 1) * SGU_CHUNK, hd * head_dim:(hd + 1) * head_dim]
            z_chunks.append(_dot(w_hd, v_blk) + bias)
        z_heads.append(jnp.concatenate(z_chunks, axis=0))
    yb = u * jnp.concatenate(z_heads, axis=-1) * _silu(gb)
    y_ref[0, out_rows, width:2 * width] = (
        _rms_scale(yb) * g_br_ref[:, width:2 * width]).astype(BF16)

    pc = _dot(h, w_in_ref[:, 5 * width:7 * width])
    q, gc = pc[:, :width], pc[:, width:]
    attn_dim = width // XATTN_HEADS
    o_heads = []
    for hd in range(XATTN_HEADS):
        cols = slice(hd * attn_dim, (hd + 1) * attn_dim)
        s = _dot(q[:, cols].astype(BF16), kt_ref[0, cols, :])
        e = jnp.exp(s - jnp.max(s, axis=-1, keepdims=True))
        p = e / jnp.sum(e, axis=-1, keepdims=True)
        o_heads.append(_dot(p.astype(BF16), v_ref[0, :, cols]))
    yc = jnp.concatenate(o_heads, axis=-1) * _silu(gc)
    y_ref[0, out_rows, 2 * width:3 * width] = (
        _rms_scale(yc) * g_br_ref[:, 2 * width:3 * width]).astype(BF16)
    return xa[rows_n - POOL_HALO:, :]


def _mixers_kernel(x_ref, g_pre_ref, w_in_hbm, pool_w_ref, pool_scale_ref, ln_g_ref, ln_b_ref,
                   sgu_w_ref, sgu_b_ref, kt_ref, v_ref, g_br_ref, y_ref,
                   halo_ref, w_in_ref, stage_ref, sem):
    tile = x_ref.shape[1]
    seq_tile = pl.program_id(1)

    @pl.when((pl.program_id(0) == 0) & (seq_tile == 0))
    def _():
        _load_weight_bf16(w_in_hbm, w_in_ref, stage_ref, sem)
        group_dim = pool_w_ref.shape[-1]
        for g in range(len(POOL_WINDOWS)):
            cols = slice(g * group_dim, (g + 1) * group_dim)
            w_in_ref[:, cols] = _dot(w_in_ref[:, cols], pool_w_ref[g].astype(BF16)).astype(BF16)

    @pl.when(seq_tile == 0)
    def _():
        halo_ref[...] = jnp.zeros_like(halo_ref)

    prev_rows = halo_ref[...]
    for row0 in range(0, tile, MIXER_ROWS):
        h = (_rms_scale(x_ref[0, row0:row0 + MIXER_ROWS, :]) * g_pre_ref[...]).astype(BF16)
        prev_rows = _mix_rows(h, prev_rows, seq_tile * tile + row0, w_in_ref, pool_scale_ref,
                              ln_g_ref, ln_b_ref, sgu_w_ref, sgu_b_ref, kt_ref, v_ref, g_br_ref,
                              y_ref, row0)
    halo_ref[...] = prev_rows


def _mixers(x, g_pre, w_in, pool_w, pool_scale, ln_g, ln_b, sgu_w, sgu_b, kt, v, g_branch):
    batch, seq, d_model = x.shape
    width = pool_scale.shape[-1]
    mem_len = v.shape[1]
    resident = pl.BlockSpec(memory_space=pltpu.VMEM)
    return pl.pallas_call(
        _mixers_kernel,
        grid=(batch, seq // MIXER_TILE),
        in_specs=[
            pl.BlockSpec((1, MIXER_TILE, d_model), lambda b, j: (b, j, 0)),
            resident,
            pl.BlockSpec(memory_space=pl.ANY),
            resident,
            resident,
            resident,
            resident,
            resident,
            resident,
            pl.BlockSpec((1, width, mem_len), lambda b, j: (b, 0, 0)),
            pl.BlockSpec((1, mem_len, width), lambda b, j: (b, 0, 0)),
            resident,
        ],
        out_specs=pl.BlockSpec((1, MIXER_TILE, 3 * width), lambda b, j: (b, j, 0)),
        out_shape=jax.ShapeDtypeStruct((batch, seq, 3 * width), BF16),
        scratch_shapes=[pltpu.VMEM((POOL_HALO, width), F32)] + _stage_scratch(*w_in.shape),
        compiler_params=pltpu.CompilerParams(
            dimension_semantics=("arbitrary", "arbitrary"), vmem_limit_bytes=CALL_VMEM_BYTES),
        name="mixers",
    )(x, g_pre, w_in, pool_w, pool_scale, ln_g, ln_b, sgu_w, sgu_b, kt, v, g_branch)


def _out_project_kernel(y_ref, x_ref, w_out_hbm, g_post_ref, o_ref, w_out_ref, stage_ref, sem):
    @pl.when(pl.program_id(0) == 0)
    def _():
        _load_weight_bf16(w_out_hbm, w_out_ref, stage_ref, sem)

    out = _dot(y_ref[...], w_out_ref[...])
    o_ref[...] = x_ref[...] + _rms_scale(out) * g_post_ref[...]


def _out_project(y, x, w_out, g_post):
    tokens, d_model = x.shape
    mix_width = y.shape[-1]
    return pl.pallas_call(
        _out_project_kernel,
        grid=(tokens // OUT_TILE,),
        in_specs=[
            pl.BlockSpec((OUT_TILE, mix_width), lambda i: (i, 0)),
            pl.BlockSpec((OUT_TILE, d_model), lambda i: (i, 0)),
            pl.BlockSpec(memory_space=pl.ANY),
            pl.BlockSpec(memory_space=pltpu.VMEM),
        ],
        out_specs=pl.BlockSpec((OUT_TILE, d_model), lambda i: (i, 0)),
        out_shape=jax.ShapeDtypeStruct((tokens, d_model), F32),
        scratch_shapes=_stage_scratch(*w_out.shape),
        compiler_params=pltpu.CompilerParams(
            dimension_semantics=("arbitrary",), vmem_limit_bytes=CALL_VMEM_BYTES),
        name="out_project",
    )(y, x, w_out, g_post)


def kernel(x, mem, norm_pre, w_in, pool_w, pool_scale, sgu_ln_g, sgu_ln_b, sgu_w, sgu_b,
           mem_norm, w_kv, branch_norm, w_out, norm_post):
    batch, seq, d_model = x.shape
    depth = w_in.shape[0]
    assert seq % MIXER_TILE == 0 and MIXER_TILE % MIXER_ROWS == 0
    assert MIXER_ROWS % SGU_CHUNK == 0 and MIXER_ROWS >= POOL_HALO
    assert (batch * seq) % OUT_TILE == 0
    row = lambda a: a.reshape(1, -1)
    for l in range(depth):
        kt, v = _memory_kv(mem, mem_norm, w_kv[l])
        y = _mixers(x, row(norm_pre[l]), w_in[l], pool_w[l], row(pool_scale[l]),
                    row(sgu_ln_g[l]), row(sgu_ln_b[l]), sgu_w[l], sgu_b[l],
                    kt, v, row(branch_norm[l]))
        x = _out_project(y.reshape(batch * seq, -1), x.reshape(batch * seq, d_model),
                         w_out[l], row(norm_post[l])).reshape(batch, seq, d_model)
    return x
```

```python
import math

import jax
import jax.numpy as jnp
from jax import lax
from jax.experimental import pallas as pl
from jax.experimental.pallas import tpu as pltpu

EPS = 1e-6
LOG2_E = 1.4426950408889634
POOL_WINDOWS = (2, 4, 8, 16)
POOL_HALO = 16
SGU_CHUNK = 128
SGU_HEADS = 8
XATTN_HEADS = 4

MIXER_TILE = 512
MIXER_ROWS = 256
OUT_TILE = 512
MIB = 1024 * 1024
V7X_VMEM_BYTES = 64 * MIB
CALL_VMEM_BYTES = V7X_VMEM_BYTES - 2 * MIB

STAGE_SLOTS = 12
STAGE_BYTES = 512 * 1024
BF16_SUBLANES = 16

F32 = jnp.float32
BF16 = jnp.bfloat16


def _rms_scale(v):
    return v * lax.rsqrt(jnp.mean(v * v, axis=-1, keepdims=True) + EPS)


def _silu(v):
    return v / (1.0 + jnp.exp2(v * -LOG2_E))


def _dot(a, b):
    return jnp.dot(a, b, preferred_element_type=F32)


def _stage_scratch(n_rows, n_cols):
    rows = max(BF16_SUBLANES, STAGE_BYTES // (4 * n_cols) // BF16_SUBLANES * BF16_SUBLANES)
    assert n_rows % rows == 0
    return [pltpu.VMEM((n_rows, n_cols), BF16),
            pltpu.VMEM((STAGE_SLOTS, rows, n_cols), F32),
            pltpu.SemaphoreType.DMA((STAGE_SLOTS,))]


def _load_weight_bf16(w_hbm, w_vmem, stage, sem):
    slots, rows = stage.shape[0], stage.shape[1]
    n_chunks = w_hbm.shape[0] // rows

    def chunk_copy(i, slot):
        return pltpu.make_async_copy(w_hbm.at[pl.ds(i * rows, rows), :], stage.at[slot], sem.at[slot])

    for i in range(min(slots - 1, n_chunks)):
        chunk_copy(i, i).start()

    def body(i, carry):
        ahead = i + slots - 1

        @pl.when(ahead < n_chunks)
        def _():
            chunk_copy(ahead, ahead % slots).start()

        slot = i % slots
        chunk_copy(i, slot).wait()
        w_vmem[pl.ds(pl.multiple_of(i * rows, rows), rows), :] = stage[slot].astype(BF16)
        return carry

    lax.fori_loop(0, n_chunks, body, 0)


def _memory_kv_kernel(mem_ref, g_ref, wkv_hbm, kt_ref, v_ref, wkv_ref, stage_ref, sem):
    @pl.when(pl.program_id(0) == 0)
    def _():
        _load_weight_bf16(wkv_hbm, wkv_ref, stage_ref, sem)

    width = v_ref.shape[-1]
    mem_n = (_rms_scale(mem_ref[0]) * g_ref[...]).astype(BF16)
    kv = _dot(mem_n, wkv_ref[...])
    k = kv[:, :width] * (1.0 / math.sqrt(width // XATTN_HEADS))
    kt_ref[0] = k.T.astype(BF16)
    v_ref[0] = kv[:, width:].astype(BF16)


def _memory_kv(mem, mem_norm, w_kv):
    batch, mem_len, d_model = mem.shape
    width = w_kv.shape[1] // 2
    resident = pl.BlockSpec(memory_space=pltpu.VMEM)
    return pl.pallas_call(
        _memory_kv_kernel,
        grid=(batch,),
        in_specs=[
            pl.BlockSpec((1, mem_len, d_model), lambda b: (b, 0, 0)),
            resident,
            pl.BlockSpec(memory_space=pl.ANY),
        ],
        out_specs=[
            pl.BlockSpec((1, width, mem_len), lambda b: (b, 0, 0)),
            pl.BlockSpec((1, mem_len, width), lambda b: (b, 0, 0)),
        ],
        out_shape=[
            jax.ShapeDtypeStruct((batch, width, mem_len), BF16),
            jax.ShapeDtypeStruct((batch, mem_len, width), BF16),
        ],
        scratch_shapes=_stage_scratch(*w_kv.shape),
        compiler_params=pltpu.CompilerParams(
            dimension_semantics=("arbitrary",), vmem_limit_bytes=CALL_VMEM_BYTES),
        name="memory_kv",
    )(mem, mem_norm.reshape(1, d_model), w_kv)


def _mix_rows(h, prev_rows, first_pos, w_in_ref, pool_scale_ref, ln_g_ref, ln_b_ref,
              sgu_w_ref, sgu_b_ref, kt_ref, v_ref, g_br_ref, y_ref, row0):
    rows_n = h.shape[0]
    width = pool_scale_ref.shape[-1]
    out_rows = pl.ds(row0, rows_n)

    pa = _dot(h, w_in_ref[:, 0:2 * width])
    xa, ga = pa[:, :width], pa[:, width:]
    rows = jnp.concatenate([prev_rows, xa], axis=0)
    seen = (first_pos + 1 + lax.broadcasted_iota(jnp.int32, (rows_n, 1), 0)).astype(F32)
    group_dim = width // len(POOL_WINDOWS)
    ya_groups = []
    for g, window in enumerate(POOL_WINDOWS):
        s = rows[:, g * group_dim:(g + 1) * group_dim]
        shift = 1
        while shift < window:
            s = s + pltpu.roll(s, shift, axis=0)
            shift *= 2
        xg = xa[:, g * group_dim:(g + 1) * group_dim]
        ya_groups.append(s[POOL_HALO:, :] / jnp.minimum(seen, float(window)) - xg)
    ya = jnp.concatenate(ya_groups, axis=-1) * pool_scale_ref[...] * _silu(ga)
    y_ref[0, out_rows, 0:width] = (_rms_scale(ya) * g_br_ref[:, 0:width]).astype(BF16)

    pb = _dot(h, w_in_ref[:, 2 * width:5 * width])
    u, vb, gb = pb[:, :width], pb[:, width:2 * width], pb[:, 2 * width:]
    vc = vb - jnp.mean(vb, axis=-1, keepdims=True)
    vn = vc * lax.rsqrt(jnp.mean(vc * vc, axis=-1, keepdims=True) + EPS)
    vn = (vn * ln_g_ref[...] + ln_b_ref[...]).astype(BF16)
    causal = (lax.broadcasted_iota(jnp.int32, (SGU_CHUNK, SGU_CHUNK), 0)
              >= lax.broadcasted_iota(jnp.int32, (SGU_CHUNK, SGU_CHUNK), 1))
    head_dim = width // SGU_HEADS
    sgu_bt = sgu_b_ref[...].T
    z_heads = []
    for hd in range(SGU_HEADS):
        w_hd = jnp.where(causal, sgu_w_ref[hd], 0.0).astype(BF16)
        bias = sgu_bt[:, hd:hd + 1]
        z_chunks = []
        for c in range(rows_n // SGU_CHUNK):
            v_blk = vn[c * SGU_CHUNK:(c + 1) * SGU_CHUNK, hd * head_dim:(hd + 1) * head_dim]
            z_chunks.append(_dot(w_hd, v_blk) + bias)
        z_heads.append(jnp.concatenate(z_chunks, axis=0))
    yb = u * jnp.concatenate(z_heads, axis=-1) * _silu(gb)
    y_ref[0, out_rows, width:2 * width] = (
        _rms_scale(yb) * g_br_ref[:, width:2 * width]).astype(BF16)

    pc = _dot(h, w_in_ref[:, 5 * width:7 * width])
    q, gc = pc[:, :width], pc[:, width:]
    attn_dim = width // XATTN_HEADS
    o_heads = []
    for hd in range(XATTN_HEADS):
        cols = slice(hd * attn_dim, (hd + 1) * attn_dim)
        s = _dot(q[:, cols].astype(BF16), kt_ref[0, cols, :])
        e = jnp.exp(s - jnp.max(s, axis=-1, keepdims=True))
        p = e / jnp.sum(e, axis=-1, keepdims=True)
        o_heads.append(_dot(p.astype(BF16), v_ref[0, :, cols]))
    yc = jnp.concatenate(o_heads, axis=-1) * _silu(gc)
    y_ref[0, out_rows, 2 * width:3 * width] = (
        _rms_scale(yc) * g_br_ref[:, 2 * width:3 * width]).astype(BF16)
    return xa[rows_n - POOL_HALO:, :]


def _mixers_kernel(x_ref, g_pre_ref, w_in_hbm, pool_w_ref, pool_scale_ref, ln_g_ref, ln_b_ref,
                   sgu_w_ref, sgu_b_ref, kt_ref, v_ref, g_br_ref, y_ref,
                   halo_ref, w_in_ref, stage_ref, sem):
    tile = x_ref.shape[1]
    seq_tile = pl.program_id(1)

    @pl.when((pl.program_id(0) == 0) & (seq_tile == 0))
    def _():
        _load_weight_bf16(w_in_hbm, w_in_ref, stage_ref, sem)
        group_dim = pool_w_ref.shape[-1]
        for g in range(len(POOL_WINDOWS)):
            cols = slice(g * group_dim, (g + 1) * group_dim)
            w_in_ref[:, cols] = _dot(w_in_ref[:, cols], pool_w_ref[g].astype(BF16)).astype(BF16)

    @pl.when(seq_tile == 0)
    def _():
        halo_ref[...] = jnp.zeros_like(halo_ref)

    prev_rows = halo_ref[...]
    for row0 in range(0, tile, MIXER_ROWS):
        h = (_rms_scale(x_ref[0, row0:row0 + MIXER_ROWS, :]) * g_pre_ref[...]).astype(BF16)
        prev_rows = _mix_rows(h, prev_rows, seq_tile * tile + row0, w_in_ref, pool_scale_ref,
                              ln_g_ref, ln_b_ref, sgu_w_ref, sgu_b_ref, kt_ref, v_ref, g_br_ref,
                              y_ref, row0)
    halo_ref[...] = prev_rows


def _mixers(x, g_pre, w_in, pool_w, pool_scale, ln_g, ln_b, sgu_w, sgu_b, kt, v, g_branch):
    batch, seq, d_model = x.shape
    width = pool_scale.shape[-1]
    mem_len = v.shape[1]
    resident = pl.BlockSpec(memory_space=pltpu.VMEM)
    return pl.pallas_call(
        _mixers_kernel,
        grid=(batch, seq // MIXER_TILE),
        in_specs=[
            pl.BlockSpec((1, MIXER_TILE, d_model), lambda b, j: (b, j, 0)),
            resident,
            pl.BlockSpec(memory_space=pl.ANY),
            resident,
            resident,
            resident,
            resident,
            resident,
            resident,
            pl.BlockSpec((1, width, mem_len), lambda b, j: (b, 0, 0)),
            pl.BlockSpec((1, mem_len, width), lambda b, j: (b, 0, 0)),
            resident,
        ],
        out_specs=pl.BlockSpec((1, MIXER_TILE, 3 * width), lambda b, j: (b, j, 0)),
        out_shape=jax.ShapeDtypeStruct((batch, seq, 3 * width), BF16),
        scratch_shapes=[pltpu.VMEM((POOL_HALO, width), F32)] + _stage_scratch(*w_in.shape),
        compiler_params=pltpu.CompilerParams(
            dimension_semantics=("arbitrary", "arbitrary"), vmem_limit_bytes=CALL_VMEM_BYTES),
        name="mixers",
    )(x, g_pre, w_in, pool_w, pool_scale, ln_g, ln_b, sgu_w, sgu_b, kt, v, g_branch)


def _out_project_kernel(y_hbm, x_hbm, w_out_hbm, g_post_ref, o_hbm, w_out_ref, stage_ref, sem):
    _load_weight_bf16(w_out_hbm, w_out_ref, stage_ref, sem)
    tokens, d_model = x_hbm.shape
    mix_width = y_hbm.shape[-1]

    def tile_body(y_ref, x_ref, o_ref):
        out = _dot(y_ref[...], w_out_ref[...])
        o_ref[...] = x_ref[...] + _rms_scale(out) * g_post_ref[...]

    pltpu.emit_pipeline(
        tile_body,
        grid=(tokens // OUT_TILE,),
        in_specs=[
            pl.BlockSpec((OUT_TILE, mix_width), lambda i: (i, 0), pipeline_mode=pl.Buffered(3)),
            pl.BlockSpec((OUT_TILE, d_model), lambda i: (i, 0), pipeline_mode=pl.Buffered(3)),
        ],
        out_specs=[pl.BlockSpec((OUT_TILE, d_model), lambda i: (i, 0))],
    )(y_hbm, x_hbm, o_hbm)


def _out_project(y, x, w_out, g_post):
    tokens, d_model = x.shape
    any_space = pl.BlockSpec(memory_space=pl.ANY)
    return pl.pallas_call(
        _out_project_kernel,
        in_specs=[
            any_space,
            any_space,
            any_space,
            pl.BlockSpec(memory_space=pltpu.VMEM),
        ],
        out_specs=any_space,
        out_shape=jax.ShapeDtypeStruct((tokens, d_model), F32),
        scratch_shapes=_stage_scratch(*w_out.shape),
        compiler_params=pltpu.CompilerParams(vmem_limit_bytes=CALL_VMEM_BYTES),
        name="out_project",
    )(y, x, w_out, g_post)


def kernel(x, mem, norm_pre, w_in, pool_w, pool_scale, sgu_ln_g, sgu_ln_b, sgu_w, sgu_b,
           mem_norm, w_kv, branch_norm, w_out, norm_post):
    batch, seq, d_model = x.shape
    depth = w_in.shape[0]
    assert seq % MIXER_TILE == 0 and MIXER_TILE % MIXER_ROWS == 0
    assert MIXER_ROWS % SGU_CHUNK == 0 and MIXER_ROWS >= POOL_HALO
    assert (batch * seq) % OUT_TILE == 0
    row = lambda a: a.reshape(1, -1)
    for l in range(depth):
        kt, v = _memory_kv(mem, mem_norm, w_kv[l])
        y = _mixers(x, row(norm_pre[l]), w_in[l], pool_w[l], row(pool_scale[l]),
                    row(sgu_ln_g[l]), row(sgu_ln_b[l]), sgu_w[l], sgu_b[l],
                    kt, v, row(branch_norm[l]))
        x = _out_project(y.reshape(batch * seq, -1), x.reshape(batch * seq, d_model),
                         w_out[l], row(norm_post[l])).reshape(batch, seq, d_model)
    return x
```

```python
import functools
import math

import jax
import jax.numpy as jnp
from jax import lax
from jax.experimental import pallas as pl
from jax.experimental.pallas import tpu as pltpu

EPS = 1e-6
LOG2_E = 1.4426950408889634
POOL_WINDOWS = (2, 4, 8, 16)
POOL_HALO = 16
SGU_CHUNK = 128
SGU_HEADS = 8
XATTN_HEADS = 4

MIXER_TILE = 512
MIXER_ROWS = 256
OUT_TILE = 512
KV_CHUNK = 512
MIB = 1024 * 1024
V7X_VMEM_BYTES = 64 * MIB
CALL_VMEM_BYTES = V7X_VMEM_BYTES - 2 * MIB

STAGE_SLOTS = 12
STAGE_BYTES = 512 * 1024
BF16_SUBLANES = 16

F32 = jnp.float32
BF16 = jnp.bfloat16


def _rms_scale(v):
    return v * lax.rsqrt(jnp.mean(v * v, axis=-1, keepdims=True) + EPS)


def _silu(v):
    return v / (1.0 + jnp.exp2(v * -LOG2_E))


def _dot(a, b):
    return jnp.dot(a, b, preferred_element_type=F32)


def _stage_scratch(n_rows, n_cols):
    rows = max(BF16_SUBLANES, STAGE_BYTES // (4 * n_cols) // BF16_SUBLANES * BF16_SUBLANES)
    assert n_rows % rows == 0
    return [pltpu.VMEM((n_rows, n_cols), BF16),
            pltpu.VMEM((STAGE_SLOTS, rows, n_cols), F32),
            pltpu.SemaphoreType.DMA((STAGE_SLOTS,))]


def _load_weight_bf16(w_hbm, w_vmem, stage, sem):
    slots, rows = stage.shape[0], stage.shape[1]
    n_chunks = w_hbm.shape[0] // rows

    def chunk_copy(i, slot):
        return pltpu.make_async_copy(w_hbm.at[pl.ds(i * rows, rows), :], stage.at[slot], sem.at[slot])

    for i in range(min(slots - 1, n_chunks)):
        chunk_copy(i, i).start()

    def body(i, carry):
        ahead = i + slots - 1

        @pl.when(ahead < n_chunks)
        def _():
            chunk_copy(ahead, ahead % slots).start()

        slot = i % slots
        chunk_copy(i, slot).wait()
        w_vmem[pl.ds(pl.multiple_of(i * rows, rows), rows), :] = stage[slot].astype(BF16)
        return carry

    lax.fori_loop(0, n_chunks, body, 0)


def _memory_kv_kernel(width, mem_ref, g_ref, wkv_ref, kt_ref, v_ref, mem_n_ref):
    chunk = pl.program_id(0)
    batch, mem_len, _ = mem_ref.shape
    n_k_chunks = width // KV_CHUNK

    @pl.when(chunk == 0)
    def _():
        for b in range(batch):
            mem_n_ref[b * mem_len:(b + 1) * mem_len, :] = (
                _rms_scale(mem_ref[b]) * g_ref[...]).astype(BF16)

    kv = _dot(mem_n_ref[...], wkv_ref[...].astype(BF16))

    @pl.when(chunk < n_k_chunks)
    def _():
        scale = 1.0 / math.sqrt(width // XATTN_HEADS)
        for b in range(batch):
            kt_ref[b] = (kv[b * mem_len:(b + 1) * mem_len, :] * scale).T.astype(BF16)

    @pl.when(chunk >= n_k_chunks)
    def _():
        for b in range(batch):
            v_ref[b] = kv[b * mem_len:(b + 1) * mem_len, :].astype(BF16)


def _memory_kv(mem, mem_norm, w_kv):
    batch, mem_len, d_model = mem.shape
    width = w_kv.shape[1] // 2
    n_k_chunks = width // KV_CHUNK
    return pl.pallas_call(
        functools.partial(_memory_kv_kernel, width),
        grid=(2 * n_k_chunks,),
        in_specs=[
            pl.BlockSpec((batch, mem_len, d_model), lambda c: (0, 0, 0)),
            pl.BlockSpec(memory_space=pltpu.VMEM),
            pl.BlockSpec((d_model, KV_CHUNK), lambda c: (0, c)),
        ],
        out_specs=[
            pl.BlockSpec((batch, KV_CHUNK, mem_len), lambda c: (0, jnp.minimum(c, n_k_chunks - 1), 0)),
            pl.BlockSpec((batch, mem_len, KV_CHUNK), lambda c: (0, 0, jnp.maximum(c - n_k_chunks, 0))),
        ],
        out_shape=[
            jax.ShapeDtypeStruct((batch, width, mem_len), BF16),
            jax.ShapeDtypeStruct((batch, mem_len, width), BF16),
        ],
        scratch_shapes=[pltpu.VMEM((batch * mem_len, d_model), BF16)],
        compiler_params=pltpu.CompilerParams(
            dimension_semantics=("arbitrary",), vmem_limit_bytes=CALL_VMEM_BYTES),
        name="memory_kv",
    )(mem, mem_norm.reshape(1, d_model), w_kv)


def _mix_rows(h, prev_rows, first_pos, w_in_ref, pool_scale_ref, ln_g_ref, ln_b_ref,
              sgu_w_ref, sgu_b_ref, kt_ref, v_ref, g_br_ref, y_ref, row0):
    rows_n = h.shape[0]
    width = pool_scale_ref.shape[-1]
    out_rows = pl.ds(row0, rows_n)

    pa = _dot(h, w_in_ref[:, 0:2 * width])
    xa, ga = pa[:, :width], pa[:, width:]
    rows = jnp.concatenate([prev_rows, xa], axis=0)
    seen = (first_pos + 1 + lax.broadcasted_iota(jnp.int32, (rows_n, 1), 0)).astype(F32)
    group_dim = width // len(POOL_WINDOWS)
    ya_groups = []
    for g, window in enumerate(POOL_WINDOWS):
        s = rows[:, g * group_dim:(g + 1) * group_dim]
        shift = 1
        while shift < window:
            s = s + pltpu.roll(s, shift, axis=0)
            shift *= 2
        xg = xa[:, g * group_dim:(g + 1) * group_dim]
        ya_groups.append(s[POOL_HALO:, :] / jnp.minimum(seen, float(window)) - xg)
    ya = jnp.concatenate(ya_groups, axis=-1) * pool_scale_ref[...] * _silu(ga)
    y_ref[0, out_rows, 0:width] = (_rms_scale(ya) * g_br_ref[:, 0:width]).astype(BF16)

    pb = _dot(h, w_in_ref[:, 2 * width:5 * width])
    u, vb, gb = pb[:, :width], pb[:, width:2 * width], pb[:, 2 * width:]
    vc = vb - jnp.mean(vb, axis=-1, keepdims=True)
    vn = vc * lax.rsqrt(jnp.mean(vc * vc, axis=-1, keepdims=True) + EPS)
    vn = (vn * ln_g_ref[...] + ln_b_ref[...]).astype(BF16)
    causal = (lax.broadcasted_iota(jnp.int32, (SGU_CHUNK, SGU_CHUNK), 0)
              >= lax.broadcasted_iota(jnp.int32, (SGU_CHUNK, SGU_CHUNK), 1))
    head_dim = width // SGU_HEADS
    sgu_bt = sgu_b_ref[...].T
    z_heads = []
    for hd in range(SGU_HEADS):
        w_hd = jnp.where(causal, sgu_w_ref[hd], 0.0).astype(BF16)
        bias = sgu_bt[:, hd:hd + 1]
        z_chunks = []
        for c in range(rows_n // SGU_CHUNK):
            v_blk = vn[c * SGU_CHUNK:(c + 1) * SGU_CHUNK, hd * head_dim:(hd + 1) * head_dim]
            z_chunks.append(_dot(w_hd, v_blk) + bias)
        z_heads.append(jnp.concatenate(z_chunks, axis=0))
    yb = u * jnp.concatenate(z_heads, axis=-1) * _silu(gb)
    y_ref[0, out_rows, width:2 * width] = (
        _rms_scale(yb) * g_br_ref[:, width:2 * width]).astype(BF16)

    pc = _dot(h, w_in_ref[:, 5 * width:7 * width])
    q, gc = pc[:, :width], pc[:, width:]
    attn_dim = width // XATTN_HEADS
    o_heads = []
    for hd in range(XATTN_HEADS):
        cols = slice(hd * attn_dim, (hd + 1) * attn_dim)
        s = _dot(q[:, cols].astype(BF16), kt_ref[0, cols, :])
        e = jnp.exp(s - jnp.max(s, axis=-1, keepdims=True))
        p = e / jnp.sum(e, axis=-1, keepdims=True)
        o_heads.append(_dot(p.astype(BF16), v_ref[0, :, cols]))
    yc = jnp.concatenate(o_heads, axis=-1) * _silu(gc)
    y_ref[0, out_rows, 2 * width:3 * width] = (
        _rms_scale(yc) * g_br_ref[:, 2 * width:3 * width]).astype(BF16)
    return xa[rows_n - POOL_HALO:, :]


def _mixers_kernel(x_ref, g_pre_ref, w_in_hbm, pool_w_ref, pool_scale_ref, ln_g_ref, ln_b_ref,
                   sgu_w_ref, sgu_b_ref, kt_ref, v_ref, g_br_ref, y_ref,
                   halo_ref, w_in_ref, stage_ref, sem):
    tile = x_ref.shape[1]
    seq_tile = pl.program_id(1)

    @pl.when((pl.program_id(0) == 0) & (seq_tile == 0))
    def _():
        _load_weight_bf16(w_in_hbm, w_in_ref, stage_ref, sem)
        group_dim = pool_w_ref.shape[-1]
        for g in range(len(POOL_WINDOWS)):
            cols = slice(g * group_dim, (g + 1) * group_dim)
            w_in_ref[:, cols] = _dot(w_in_ref[:, cols], pool_w_ref[g].astype(BF16)).astype(BF16)

    @pl.when(seq_tile == 0)
    def _():
        halo_ref[...] = jnp.zeros_like(halo_ref)

    prev_rows = halo_ref[...]
    for row0 in range(0, tile, MIXER_ROWS):
        h = (_rms_scale(x_ref[0, row0:row0 + MIXER_ROWS, :]) * g_pre_ref[...]).astype(BF16)
        prev_rows = _mix_rows(h, prev_rows, seq_tile * tile + row0, w_in_ref, pool_scale_ref,
                              ln_g_ref, ln_b_ref, sgu_w_ref, sgu_b_ref, kt_ref, v_ref, g_br_ref,
                              y_ref, row0)
    halo_ref[...] = prev_rows


def _mixers(x, g_pre, w_in, pool_w, pool_scale, ln_g, ln_b, sgu_w, sgu_b, kt, v, g_branch):
    batch, seq, d_model = x.shape
    width = pool_scale.shape[-1]
    mem_len = v.shape[1]
    resident = pl.BlockSpec(memory_space=pltpu.VMEM)
    return pl.pallas_call(
        _mixers_kernel,
        grid=(batch, seq // MIXER_TILE),
        in_specs=[
            pl.BlockSpec((1, MIXER_TILE, d_model), lambda b, j: (b, j, 0)),
            resident,
            pl.BlockSpec(memory_space=pl.ANY),
            resident,
            resident,
            resident,
            resident,
            resident,
            resident,
            pl.BlockSpec((1, width, mem_len), lambda b, j: (b, 0, 0)),
            pl.BlockSpec((1, mem_len, width), lambda b, j: (b, 0, 0)),
            resident,
        ],
        out_specs=pl.BlockSpec((1, MIXER_TILE, 3 * width), lambda b, j: (b, j, 0)),
        out_shape=jax.ShapeDtypeStruct((batch, seq, 3 * width), BF16),
        scratch_shapes=[pltpu.VMEM((POOL_HALO, width), F32)] + _stage_scratch(*w_in.shape),
        compiler_params=pltpu.CompilerParams(
            dimension_semantics=("arbitrary", "arbitrary"), vmem_limit_bytes=CALL_VMEM_BYTES),
        name="mixers",
    )(x, g_pre, w_in, pool_w, pool_scale, ln_g, ln_b, sgu_w, sgu_b, kt, v, g_branch)


def _out_project_kernel(y_ref, x_ref, w_out_hbm, g_post_ref, o_ref, w_out_ref, stage_ref, sem):
    @pl.when(pl.program_id(0) == 0)
    def _():
        _load_weight_bf16(w_out_hbm, w_out_ref, stage_ref, sem)

    out = _dot(y_ref[...], w_out_ref[...])
    o_ref[...] = x_ref[...] + _rms_scale(out) * g_post_ref[...]


def _out_project(y, x, w_out, g_post):
    tokens, d_model = x.shape
    mix_width = y.shape[-1]
    return pl.pallas_call(
        _out_project_kernel,
        grid=(tokens // OUT_TILE,),
        in_specs=[
            pl.BlockSpec((OUT_TILE, mix_width), lambda i: (i, 0)),
            pl.BlockSpec((OUT_TILE, d_model), lambda i: (i, 0)),
            pl.BlockSpec(memory_space=pl.ANY),
            pl.BlockSpec(memory_space=pltpu.VMEM),
        ],
        out_specs=pl.BlockSpec((OUT_TILE, d_model), lambda i: (i, 0)),
        out_shape=jax.ShapeDtypeStruct((tokens, d_model), F32),
        scratch_shapes=_stage_scratch(*w_out.shape),
        compiler_params=pltpu.CompilerParams(
            dimension_semantics=("arbitrary",), vmem_limit_bytes=CALL_VMEM_BYTES),
        name="out_project",
    )(y, x, w_out, g_post)


def kernel(x, mem, norm_pre, w_in, pool_w, pool_scale, sgu_ln_g, sgu_ln_b, sgu_w, sgu_b,
           mem_norm, w_kv, branch_norm, w_out, norm_post):
    batch, seq, d_model = x.shape
    depth = w_in.shape[0]
    assert seq % MIXER_TILE == 0 and MIXER_TILE % MIXER_ROWS == 0
    assert MIXER_ROWS % SGU_CHUNK == 0 and MIXER_ROWS >= POOL_HALO
    assert (batch * seq) % OUT_TILE == 0
    row = lambda a: a.reshape(1, -1)
    for l in range(depth):
        kt, v = _memory_kv(mem, mem_norm, w_kv[l])
        y = _mixers(x, row(norm_pre[l]), w_in[l], pool_w[l], row(pool_scale[l]),
                    row(sgu_ln_g[l]), row(sgu_ln_b[l]), sgu_w[l], sgu_b[l],
                    kt, v, row(branch_norm[l]))
        x = _out_project(y.reshape(batch * seq, -1), x.reshape(batch * seq, d_model),
                         w_out[l], row(norm_post[l])).reshape(batch, seq, d_model)
    return x
```
